```python
import jax, jax.numpy as jnp
from jax import lax
import numpy as np

D_MODEL = 1024
BATCH = 32
SEQ = 2048
DEPTH = 1
DEC_BATCH = 128
DEC_SEQ = 4
PAST_LEN = 16384
PAGE_SIZE = 128

CONV_CH = 512
CONV_WIDTH = 31
N_HEADS = 8
QK_NOPE_DIM = 64
QK_ROPE_DIM = 32
QK_HEAD_DIM = QK_NOPE_DIM + QK_ROPE_DIM
V_HEAD_DIM = 64
Q_LORA = 384
KV_LORA = 256
ROPE_THETA = 10000.0
SM_SCALE = QK_HEAD_DIM ** -0.5
Q_BLOCK = 128
N_EXPERTS = 32
TOP_K = 4
D_FF = 1024
SWIGLU_LIMIT = 7.0
SWIGLU_ALPHA = 1.702
EXPERT_BLOCK = 128
N_MOD = 6
EPS = 1e-6
OFF_Q = 2 * CONV_CH
OFF_KV = OFF_Q + Q_LORA
OFF_KR = OFF_KV + KV_LORA
OFF_GA = OFF_KR + QK_ROPE_DIM
OFF_GB = OFF_GA + D_MODEL
IN_TOTAL = OFF_GB + D_MODEL

kernel_name = 'hybrid_conformer_mla_moe_adaln_step'


def rms_norm(x, g):
    xf = x.astype(jnp.float32)
    y = xf * lax.rsqrt(jnp.mean(xf * xf, axis=-1, keepdims=True) + EPS)
    return (y * g.astype(jnp.float32)).astype(x.dtype)


def layer_norm(x, g, b):
    xf = x.astype(jnp.float32)
    mu = jnp.mean(xf, axis=-1, keepdims=True)
    var = jnp.mean(jnp.square(xf - mu), axis=-1, keepdims=True)
    y = (xf - mu) * lax.rsqrt(var + EPS)
    return (y * g.astype(jnp.float32) + b.astype(jnp.float32)).astype(x.dtype)


def rope_cos_sin(pos):
    inv_freq = ROPE_THETA ** (-jnp.arange(0, QK_ROPE_DIM, 2, dtype=jnp.float32) / QK_ROPE_DIM)
    ang = pos.astype(jnp.float32)[..., None] * inv_freq
    return jnp.cos(ang), jnp.sin(ang)


def apply_rope(x, cos, sin):
    half = QK_ROPE_DIM // 2
    xf = x.astype(jnp.float32)
    x1, x2 = xf[..., :half], xf[..., half:]
    return jnp.concatenate([x1 * cos - x2 * sin, x2 * cos + x1 * sin], axis=-1).astype(x.dtype)


def conv_module(u, conv_state, w_dw, b_dw, g_ln, b_ln, w_pw):
    a, g = jnp.split(u, 2, axis=-1)
    v = a * jax.nn.sigmoid(g)
    full = jnp.concatenate([conv_state.astype(v.dtype), v], axis=1)
    y = lax.conv_general_dilated(full, w_dw[:, None, :].astype(v.dtype), window_strides=(1,), padding='VALID',
                                 dimension_numbers=('NWC', 'WIO', 'NWC'), feature_group_count=CONV_CH)
    y = jax.nn.silu(layer_norm(y + b_dw, g_ln, b_ln))
    return y @ w_pw, full[:, full.shape[1] - (CONV_WIDTH - 1):]


def mla_queries(q_lat, g_ql, w_uq, g_qh, cos, sin):
    q = jnp.einsum('btr,rhd->bthd', rms_norm(q_lat, g_ql), w_uq)
    q_rot = apply_rope(q[..., QK_NOPE_DIM:], cos[:, :, None, :], sin[:, :, None, :])
    return rms_norm(jnp.concatenate([q[..., :QK_NOPE_DIM], q_rot], axis=-1), g_qh)


def mla_keys_values(c_kv, k_pe, w_uk, w_uv, g_kh):
    k_nope = jnp.einsum('...sc,chd->...shd', c_kv, w_uk)
    k_rot = jnp.broadcast_to(k_pe[..., None, :], k_nope.shape[:-1] + (QK_ROPE_DIM,)).astype(k_nope.dtype)
    k = rms_norm(jnp.concatenate([k_nope, k_rot], axis=-1), g_kh)
    v = jnp.einsum('...sc,chd->...shd', c_kv, w_uv)
    return k, v


def attend_prompt(q, c_kv, k_pe, w_uk, w_uv, g_kh):
    b, s = q.shape[0], q.shape[1]
    k, v = mla_keys_values(c_kv, k_pe, w_uk, w_uv, g_kh)
    n_qb = s // Q_BLOCK
    qb = q.reshape(b, n_qb, Q_BLOCK, N_HEADS, QK_HEAD_DIM).transpose(1, 0, 2, 3, 4)
    kpos = jnp.arange(s)

    def one_block(args):
        q_blk, i = args
        qpos = i * Q_BLOCK + jnp.arange(Q_BLOCK)
        sc = jnp.einsum('bqhd,bkhd->bhqk', q_blk, k).astype(jnp.float32) * SM_SCALE
        sc = jnp.where(kpos[None, :] <= qpos[:, None], sc, -jnp.inf)
        pr = jax.nn.softmax(sc, axis=-1).astype(v.dtype)
        return jnp.einsum('bhqk,bkhd->bqhd', pr, v)

    o = lax.map(one_block, (qb, jnp.arange(n_qb)))
    return o.transpose(1, 0, 2, 3, 4).reshape(b, s, N_HEADS, V_HEAD_DIM)


def make_paged_attend(pool_c, pool_pe, page_table, past_len):
    def attend(q, c_kv, k_pe, w_uk, w_uv, g_kh):
        t = q.shape[1]
        qpos = past_len + jnp.arange(t)
        kpos = jnp.arange(past_len + t)
        mask = kpos[None, :] <= qpos[:, None]

        def one_seq(args):
            q_s, c_new, pe_new, pages = args
            c_all = jnp.concatenate([pool_c[pages].reshape(-1, KV_LORA).astype(c_new.dtype), c_new], axis=0)
            pe_all = jnp.concatenate([pool_pe[pages].reshape(-1, QK_ROPE_DIM).astype(pe_new.dtype), pe_new], axis=0)
            k, v = mla_keys_values(c_all, pe_all, w_uk, w_uv, g_kh)
            sc = jnp.einsum('qhd,khd->hqk', q_s, k).astype(jnp.float32) * SM_SCALE
            sc = jnp.where(mask[None], sc, -jnp.inf)
            pr = jax.nn.softmax(sc, axis=-1).astype(v.dtype)
            return jnp.einsum('hqk,khd->qhd', pr, v)

        return lax.map(one_seq, (q, c_kv, k_pe, page_table))
    return attend


def clamped_swiglu(gu):
    gate, lin = jnp.split(gu, 2, axis=-1)
    gate = jnp.minimum(gate, SWIGLU_LIMIT)
    lin = jnp.clip(lin, -SWIGLU_LIMIT, SWIGLU_LIMIT)
    return gate * jax.nn.sigmoid(SWIGLU_ALPHA * gate) * (lin + 1)


def moe_ffn(h, w_r, b_r, w_up, b_up, w_dn, b_dn):
    lead = h.shape[:-1]
    x = h.reshape(-1, D_MODEL)
    n_tok = x.shape[0]
    n_asg = n_tok * TOP_K
    logits = (x @ w_r).astype(jnp.float32) + b_r.astype(jnp.float32)
    top_logits, top_idx = lax.top_k(logits, TOP_K)
    probs = jax.nn.softmax(top_logits, axis=-1)
    flat_e = top_idx.reshape(-1)
    order = jnp.argsort(flat_e)
    e_sorted = flat_e[order]
    tok_sorted = order // TOP_K
    w_sorted = probs.reshape(-1)[order].astype(x.dtype)
    counts = jnp.bincount(flat_e, length=N_EXPERTS)
    padded = (counts + EXPERT_BLOCK - 1) // EXPERT_BLOCK * EXPERT_BLOCK
    pad_end = jnp.cumsum(padded)
    pad_start = pad_end - padded
    start = jnp.cumsum(counts) - counts
    dest = pad_start[e_sorted] + jnp.arange(n_asg) - start[e_sorted]
    n_blocks = -(-n_asg // EXPERT_BLOCK) + N_EXPERTS
    buf = jnp.zeros((n_blocks * EXPERT_BLOCK, D_MODEL), x.dtype).at[dest].set(x[tok_sorted])
    block_expert = jnp.minimum(jnp.searchsorted(pad_end, jnp.arange(n_blocks) * EXPERT_BLOCK, side='right'), N_EXPERTS - 1)

    def expert_block(args):
        xb, e = args
        act = clamped_swiglu(xb @ w_up[e] + b_up[e])
        return act @ w_dn[e] + b_dn[e]

    y_buf = lax.map(expert_block, (buf.reshape(n_blocks, EXPERT_BLOCK, D_MODEL), block_expert))
    y_rows = y_buf.reshape(-1, D_MODEL)[dest] * w_sorted[:, None]
    y = jax.ops.segment_sum(y_rows, tok_sorted, num_segments=n_tok)
    return y.reshape(lead + (D_MODEL,)).astype(h.dtype)


def decoder_layer(x, c, pos, conv_state, attend, p):
    (w_ada, b_ada, g_mix, g_ffn, w_in, w_dw, b_dw, g_ln, b_ln, w_co, g_ql, w_uq, g_kvl, w_uk, w_uv,
     g_qh, g_kh, w_ao, w_out, w_r, b_r, w_up, b_up, w_dn, b_dn) = p
    b, t = x.shape[0], x.shape[1]
    mod = (jax.nn.silu(c) @ w_ada + b_ada).reshape(b, N_MOD, 1, D_MODEL)
    shift1, scale1, gate1, shift2, scale2, gate2 = (mod[:, i] for i in range(N_MOD))
    h = rms_norm(x, g_mix) * (1 + scale1) + shift1
    proj = h @ w_in
    u_conv = proj[..., :OFF_Q]
    q_lat = proj[..., OFF_Q:OFF_KV]
    kv_lat = proj[..., OFF_KV:OFF_KR]
    k_pe_raw = proj[..., OFF_KR:OFF_GA]
    gate_a = proj[..., OFF_GA:OFF_GB]
    gate_b = proj[..., OFF_GB:]
    conv_out, new_conv = conv_module(u_conv, conv_state, w_dw, b_dw, g_ln, b_ln, w_co)
    cos, sin = rope_cos_sin(pos)
    q = mla_queries(q_lat, g_ql, w_uq, g_qh, cos, sin)
    c_kv = rms_norm(kv_lat, g_kvl)
    k_pe = apply_rope(k_pe_raw, cos, sin)
    attn = attend(q, c_kv, k_pe, w_uk, w_uv, g_kh)
    attn_out = attn.reshape(b, t, N_HEADS * V_HEAD_DIM) @ w_ao
    merged = jax.nn.sigmoid(gate_a) * conv_out + jax.nn.sigmoid(gate_b) * attn_out
    x = x + gate1 * (merged @ w_out)
    h2 = rms_norm(x, g_ffn) * (1 + scale2) + shift2
    x = x + gate2 * moe_ffn(h2, w_r, b_r, w_up, b_up, w_dn, b_dn)
    return x, c_kv, k_pe, new_conv


def setup_inputs(seed: int = 0) -> dict:
    key = jax.random.key(seed)
    keys = jax.random.split(key, 48)
    cnt = [0]

    def nxt():
        k = keys[cnt[0]]
        cnt[0] += 1
        return k

    def nrm(shape, scale):
        return scale * jax.random.normal(nxt(), shape, jnp.float32)

    def gain(shape):
        return 1.0 + nrm(shape, 0.05)

    n_pages = PAST_LEN // PAGE_SIZE
    n_phys = (DEC_BATCH * n_pages * 5) // 4
    L = DEPTH
    D = D_MODEL
    out = {}
    out['x_prompt'] = nrm((BATCH, SEQ, D), 1.0)
    out['x_sample'] = nrm((DEC_BATCH, DEC_SEQ, D), 1.0)
    out['cache_kv_latent'] = nrm((L, n_phys, PAGE_SIZE, KV_LORA), 1.0)
    out['cache_k_rope'] = nrm((L, n_phys, PAGE_SIZE, QK_ROPE_DIM), 1.0)
    out['state_conv'] = nrm((L, DEC_BATCH, CONV_WIDTH - 1, CONV_CH), 0.5)
    out['page_table'] = jax.random.permutation(nxt(), n_phys)[:DEC_BATCH * n_pages].reshape(DEC_BATCH, n_pages).astype(jnp.int32)
    out['c_prompt'] = nrm((BATCH, D), 1.0)
    out['c_sample'] = nrm((DEC_BATCH, D), 1.0)
    out['w_ada'] = nrm((L, D, N_MOD * D), 0.5 * D ** -0.5)
    out['b_ada'] = nrm((L, N_MOD * D), 0.02)
    out['g_norm_mix'] = gain((L, D))
    out['g_norm_ffn'] = gain((L, D))
    out['w_in'] = nrm((L, D, IN_TOTAL), D ** -0.5)
    out['w_dw'] = nrm((L, CONV_WIDTH, CONV_CH), CONV_WIDTH ** -0.5)
    out['b_dw'] = nrm((L, CONV_CH), 0.02)
    out['g_conv_ln'] = gain((L, CONV_CH))
    out['b_conv_ln'] = nrm((L, CONV_CH), 0.02)
    out['w_conv_out'] = nrm((L, CONV_CH, D), CONV_CH ** -0.5)
    out['g_q_lat'] = gain((L, Q_LORA))
    out['w_uq'] = nrm((L, Q_LORA, N_HEADS, QK_HEAD_DIM), Q_LORA ** -0.5)
    out['g_kv_lat'] = gain((L, KV_LORA))
    out['w_uk'] = nrm((L, KV_LORA, N_HEADS, QK_NOPE_DIM), KV_LORA ** -0.5)
    out['w_uv'] = nrm((L, KV_LORA, N_HEADS, V_HEAD_DIM), KV_LORA ** -0.5)
    out['g_q_head'] = gain((L, QK_HEAD_DIM))
    out['g_k_head'] = gain((L, QK_HEAD_DIM))
    out['w_attn_out'] = nrm((L, N_HEADS * V_HEAD_DIM, D), (N_HEADS * V_HEAD_DIM) ** -0.5)
    out['w_out'] = nrm((L, D, D), D ** -0.5)
    out['w_router'] = nrm((L, D, N_EXPERTS), D ** -0.5)
    out['b_router'] = nrm((L, N_EXPERTS), 0.01)
    out['w_up'] = nrm((L, N_EXPERTS, D, 2 * D_FF), D ** -0.5)
    out['b_up'] = nrm((L, N_EXPERTS, 2 * D_FF), 0.01)
    out['w_down'] = nrm((L, N_EXPERTS, D_FF, D), D_FF ** -0.5)
    out['b_down'] = nrm((L, N_EXPERTS, D), 0.01)
    return out


def reference(x_prompt, x_sample, cache_kv_latent, cache_k_rope, state_conv, page_table, c_prompt, c_sample,
              w_ada, b_ada, g_norm_mix, g_norm_ffn, w_in, w_dw, b_dw, g_conv_ln, b_conv_ln, w_conv_out,
              g_q_lat, w_uq, g_kv_lat, w_uk, w_uv, g_q_head, g_k_head, w_attn_out, w_out,
              w_router, b_router, w_up, b_up, w_down, b_down):
    past_len = page_table.shape[1] * PAGE_SIZE
    pos_p = jnp.arange(x_prompt.shape[1], dtype=jnp.int32)[None]
    pos_s = past_len + jnp.arange(x_sample.shape[1], dtype=jnp.int32)[None]
    yp, ys = x_prompt, x_sample
    kv_p, kr_p, cv_p, kv_s, kr_s, cv_s = [], [], [], [], [], []
    for l in range(DEPTH):
        p = (w_ada[l], b_ada[l], g_norm_mix[l], g_norm_ffn[l], w_in[l], w_dw[l], b_dw[l], g_conv_ln[l],
             b_conv_ln[l], w_conv_out[l], g_q_lat[l], w_uq[l], g_kv_lat[l], w_uk[l], w_uv[l], g_q_head[l],
             g_k_head[l], w_attn_out[l], w_out[l], w_router[l], b_router[l], w_up[l], b_up[l], w_down[l], b_down[l])
        zero_state = jnp.zeros((x_prompt.shape[0], CONV_WIDTH - 1, CONV_CH), x_prompt.dtype)
        yp, c_new_p, r_new_p, s_new_p = decoder_layer(yp, c_prompt, pos_p, zero_state, attend_prompt, p)
        attend_s = make_paged_attend(cache_kv_latent[l], cache_k_rope[l], page_table, past_len)
        ys, c_new_s, r_new_s, s_new_s = decoder_layer(ys, c_sample, pos_s, state_conv[l], attend_s, p)
        kv_p.append(c_new_p)
        kr_p.append(r_new_p)
        cv_p.append(s_new_p)
        kv_s.append(c_new_s)
        kr_s.append(r_new_s)
        cv_s.append(s_new_s)
    return (yp, ys, jnp.stack(kv_p), jnp.stack(kr_p), jnp.stack(cv_p), jnp.stack(kv_s), jnp.stack(kr_s), jnp.stack(cv_s))
```

```python
import functools

import jax
import jax.numpy as jnp
from jax import lax
from jax.experimental import pallas as pl
from jax.experimental.pallas import tpu as pltpu

F32 = jnp.float32
BF16 = jnp.bfloat16
I32 = jnp.int32

EPS = 1e-6
TOP_K = 4
ROPE_THETA = 10000.0
SWIGLU_LIMIT = 7.0
SWIGLU_ALPHA = 1.702
N_MOD = 6

LANE = 128
SUBLANE = 8
VMEM_LIMIT_BYTES = 56 * 1024 * 1024
NEG = -1e30


def _cp(*sem):
    return pltpu.CompilerParams(dimension_semantics=sem, vmem_limit_bytes=VMEM_LIMIT_BYTES)


def _tile(n, pref, mult=SUBLANE):
    t = min(pref, n)
    t -= t % mult
    while t >= mult:
        if n % t == 0:
            return t
        t -= mult
    return n


def _sigmoid(x):
    return 1.0 / (1.0 + jnp.exp(-x))


def _rms(x, g):
    return x * lax.rsqrt(jnp.mean(x * x, axis=-1, keepdims=True) + EPS) * g


def _dot(a, b):
    return jnp.dot(a, b, preferred_element_type=F32)


def _adaln_kernel(c_ref, w_ref, b_ref, o_ref):
    c = c_ref[...]
    s = (c * _sigmoid(c)).astype(BF16)
    o_ref[...] = _dot(s, w_ref[...].astype(BF16)) + b_ref[...]


def _adaln(c, w_ada, b_ada):
    r, d = c.shape
    n = w_ada.shape[1]
    tn = _tile(n, 1536, LANE)
    return pl.pallas_call(
        _adaln_kernel,
        out_shape=jax.ShapeDtypeStruct((r, n), F32),
        grid=(n // tn,),
        in_specs=[
            pl.BlockSpec((r, d), lambda j: (0, 0)),
            pl.BlockSpec((d, tn), lambda j: (0, j)),
            pl.BlockSpec((1, tn), lambda j: (0, j)),
        ],
        out_specs=pl.BlockSpec((r, tn), lambda j: (0, j)),
        compiler_params=_cp("arbitrary"),
        name="adaln",
    )(c, w_ada, b_ada.reshape(1, n))


def _inproj_kernel(x_ref, sh_ref, sc_ref, cq_ref, sq_ref, ck_ref, sk_ref, gmix_ref, w1_ref, gql_ref, wq_ref,
                   gqh_ref, gkvl_ref, wuk_ref, epe_ref, gkh_ref, wuv_ref,
                   v_ref, q_ref, ckv_ref, kpe_ref, k_ref, vv_ref, *, c, ql, kl, h, qk, rope):
    x = x_ref[...]
    hmod = _rms(x, gmix_ref[...]) * (1.0 + sc_ref[0]) + sh_ref[0]
    p = _dot(hmod.astype(BF16), w1_ref[...])
    v_ref[...] = p[:, :c] * _sigmoid(p[:, c:2 * c])
    o_q = 2 * c
    o_kv = o_q + ql
    o_kr = o_kv + kl
    qn = _rms(p[:, o_q:o_kv], gql_ref[...])
    qq = _dot(qn.astype(BF16), wq_ref[...])
    cq = cq_ref[...]
    sq = sq_ref[...]
    inv_qk = 1.0 / qk
    for hd in range(h):
        qr = qq[:, hd * LANE:(hd + 1) * LANE] * cq + qq[:, (h + hd) * LANE:(h + hd + 1) * LANE] * sq
        ms = jnp.sum(qr * qr, axis=-1, keepdims=True) * inv_qk
        q_ref[:, hd * LANE:(hd + 1) * LANE] = (qr * lax.rsqrt(ms + EPS) * gqh_ref[...]).astype(q_ref.dtype)
    ckv = _rms(p[:, o_kv:o_kr], gkvl_ref[...])
    ckv_ref[...] = ckv
    kpe = p[:, o_kr:o_kr + LANE] * ck_ref[...] + p[:, o_kr + LANE:o_kr + 2 * LANE] * sk_ref[...]
    kpe_ref[...] = kpe[:, :rope]
    cb = ckv.astype(BF16)
    kf = _dot(cb, wuk_ref[...]) + _dot(kpe.astype(BF16), epe_ref[...])
    for hd in range(h):
        kh = kf[:, hd * LANE:(hd + 1) * LANE]
        ms = jnp.sum(kh * kh, axis=-1, keepdims=True) * inv_qk
        k_ref[:, hd * LANE:(hd + 1) * LANE] = (kh * lax.rsqrt(ms + EPS) * gkh_ref[...]).astype(k_ref.dtype)
    vv_ref[...] = _dot(cb, wuv_ref[...]).astype(vv_ref.dtype)


def _inproj(x, mods, mod_map, tabs, tab_map, wp, dims, tm):
    n, d = x.shape
    c, ql, kl, h, qk, rope, vd = dims["c"], dims["ql"], dims["kl"], dims["h"], dims["qk"], dims["rope"], dims["vd"]
    mblk = (1,) + mods[0].shape[1:]
    full = lambda a: pl.BlockSpec(a.shape, lambda i: (0,) * a.ndim)
    row = lambda w: pl.BlockSpec((tm, w), lambda i: (i, 0))
    tab = pl.BlockSpec((tm, LANE), tab_map)
    weights = [wp["gmix"], wp["w1"], wp["gql"], wp["wq"], wp["gqh"], wp["gkvl"], wp["wuk"], wp["epe"], wp["gkh"],
               wp["wuv"]]
    return pl.pallas_call(
        functools.partial(_inproj_kernel, c=c, ql=ql, kl=kl, h=h, qk=qk, rope=rope),
        out_shape=(
            jax.ShapeDtypeStruct((n, c), F32),
            jax.ShapeDtypeStruct((n, h * LANE), BF16),
            jax.ShapeDtypeStruct((n, kl), F32),
            jax.ShapeDtypeStruct((n, rope), F32),
            jax.ShapeDtypeStruct((n, h * LANE), BF16),
            jax.ShapeDtypeStruct((n, h * vd), BF16),
        ),
        grid=(n // tm,),
        in_specs=[row(d), pl.BlockSpec(mblk, mod_map), pl.BlockSpec(mblk, mod_map), tab, tab, tab, tab]
        + [full(w) for w in weights],
        out_specs=(row(c), row(h * LANE), row(kl), row(rope), row(h * LANE), row(h * vd)),
        compiler_params=_cp("parallel"),
        name="inproj",
    )(x, mods[0], mods[1], *tabs, *weights)


def _conv_kernel(v_ref, st_ref, wdw_ref, bdw_ref, gln_ref, bln_ref, y_ref, ns_ref, win_ref, *, width, tt, tv, rb,
                 halo):
    j = pl.program_id(1)
    off = halo - (width - 1)

    @pl.when(j == 0)
    def _():
        win_ref[pl.ds(off, width - 1), :] = st_ref[0]

    @pl.when(j > 0)
    def _():
        win_ref[pl.ds(off, width - 1), :] = win_ref[pl.ds(tt + off, width - 1), :]

    win_ref[pl.ds(halo, tt), :] = v_ref[0]
    wdw = wdw_ref[...]
    bdw = bdw_ref[...]
    gln = gln_ref[...]
    bln = bln_ref[...]

    def body(i, carry):
        r0 = pl.multiple_of(i * rb, rb)
        chunk = win_ref[pl.ds(r0, rb + halo), :]
        acc = bdw + wdw[0:1, :] * chunk[off:off + rb, :]
        for k in range(1, width):
            acc = acc + wdw[k:k + 1, :] * chunk[off + k:off + k + rb, :]
        mu = jnp.mean(acc, axis=-1, keepdims=True)
        dlt = acc - mu
        var = jnp.mean(dlt * dlt, axis=-1, keepdims=True)
        yn = dlt * lax.rsqrt(var + EPS) * gln + bln
        y_ref[0, pl.ds(r0, rb), :] = (yn * _sigmoid(yn)).astype(y_ref.dtype)
        return carry

    lax.fori_loop(0, tt // rb, body, 0)

    @pl.when(j == pl.num_programs(1) - 1)
    def _():
        ns_ref[0] = win_ref[pl.ds(tv + off, width - 1), :]


def _conv(v, state, w_dw, b_dw, g_ln, b_ln, t_valid):
    b, t, c = v.shape
    width = w_dw.shape[0]
    halo = -(-(width - 1) // SUBLANE) * SUBLANE
    tt = _tile(t, 512)
    rb = _tile(tt, 16)
    nt = t // tt
    tv = t_valid - (nt - 1) * tt
    vec = lambda a: a.reshape(1, c)
    cst = lambda shape: pl.BlockSpec(shape, lambda i, j: (0, 0))
    return pl.pallas_call(
        functools.partial(_conv_kernel, width=width, tt=tt, tv=tv, rb=rb, halo=halo),
        out_shape=(jax.ShapeDtypeStruct((b, t, c), BF16), jax.ShapeDtypeStruct((b, width - 1, c), F32)),
        grid=(b, nt),
        in_specs=[
            pl.BlockSpec((1, tt, c), lambda i, j: (i, j, 0)),
            pl.BlockSpec((1, width - 1, c), lambda i, j: (i, 0, 0)),
            cst((width, c)), cst((1, c)), cst((1, c)), cst((1, c)),
        ],
        out_specs=(
            pl.BlockSpec((1, tt, c), lambda i, j: (i, j, 0)),
            pl.BlockSpec((1, width - 1, c), lambda i, j: (i, 0, 0)),
        ),
        scratch_shapes=[pltpu.VMEM((tt + halo, c), F32)],
        compiler_params=_cp("parallel", "arbitrary"),
        name="conv",
    )(v, state, w_dw, vec(b_dw), vec(g_ln), vec(b_ln))


def _attn_kernel(q_ref, k_ref, v_ref, o_ref, *, tq, vd):
    qi = pl.program_id(2)
    q = q_ref[0]
    row = lax.broadcasted_iota(I32, (tq, tq), 0)
    col = lax.broadcasted_iota(I32, (tq, tq), 1)

    def block(j, carry, masked):
        ks = pl.multiple_of(j * tq, tq)
        kb = k_ref[0, pl.ds(ks, tq), :]
        vb = v_ref[0, pl.ds(ks, tq), :]
        out = []
        for hh in range(2):
            m, l, acc = carry[hh]
            s = lax.dot_general(q[:, hh * LANE:(hh + 1) * LANE], kb[:, hh * LANE:(hh + 1) * LANE],
                                (((1,), (1,)), ((), ())), preferred_element_type=F32)
            if masked:
                s = jnp.where(col <= row, s, NEG)
            m_new = jnp.maximum(m, jnp.max(s, axis=-1, keepdims=True))
            alpha = jnp.exp(m - m_new)
            p = jnp.exp(s - m_new)
            l = alpha * l + jnp.sum(p, axis=-1, keepdims=True)
            acc = alpha * acc + _dot(p.astype(BF16), vb)
            out.append((m_new, l, acc))
        return tuple(out)

    one = (jnp.full((tq, 1), NEG, F32), jnp.zeros((tq, 1), F32), jnp.zeros((tq, LANE), F32))
    carry = lax.fori_loop(0, qi, lambda j, cr: block(j, cr, False), (one, one))
    carry = block(qi, carry, True)
    o0 = carry[0][2] / carry[0][1]
    o1 = carry[1][2] / carry[1][1]
    lane = lax.broadcasted_iota(I32, (tq, LANE), 1)
    o_ref[0] = jnp.where(lane < vd, o0, o1).astype(o_ref.dtype)


def _attention(q, k, v, b, s, h, vd):
    assert 2 * vd == LANE and h % 2 == 0
    tq = _tile(s, 512)
    q3 = q.reshape(b, s, h * LANE)
    k3 = k.reshape(b, s, h * LANE)
    v3 = v.reshape(b, s, h * vd)
    out = pl.pallas_call(
        functools.partial(_attn_kernel, tq=tq, vd=vd),
        out_shape=jax.ShapeDtypeStruct((b, s, h * vd), BF16),
        grid=(b, h // 2, s // tq),
        in_specs=[
            pl.BlockSpec((1, tq, 2 * LANE), lambda i, p, j: (i, j, p)),
            pl.BlockSpec((1, s, 2 * LANE), lambda i, p, j: (i, 0, p)),
            pl.BlockSpec((1, s, LANE), lambda i, p, j: (i, 0, p)),
        ],
        out_specs=pl.BlockSpec((1, tq, LANE), lambda i, p, j: (i, j, p)),
        compiler_params=_cp("parallel", "parallel", "arbitrary"),
        name="attention",
    )(q3, k3, v3)
    return out.reshape(b * s, h * vd)


def _absorb_kernel(q_ref, gkh_ref, wukt_ref, sel_ref, a_ref, *, h, kl):
    aw = kl + LANE
    for hd in range(h):
        qg = (q_ref[:, hd * LANE:(hd + 1) * LANE].astype(F32) * gkh_ref[...]).astype(BF16)
        a_ref[:, hd * aw:hd * aw + kl] = _dot(qg, wukt_ref[hd])
        a_ref[:, hd * aw + kl:(hd + 1) * aw] = _dot(qg, sel_ref[...])


def _absorb(q, gkh_pad, wukt, sel, h, kl):
    n = q.shape[0]
    full = lambda a: pl.BlockSpec(a.shape, lambda i: (0,) * a.ndim)
    return pl.pallas_call(
        functools.partial(_absorb_kernel, h=h, kl=kl),
        out_shape=jax.ShapeDtypeStruct((n, h * (kl + LANE)), F32),
        grid=(1,),
        in_specs=[full(q), full(gkh_pad), full(wukt), full(sel)],
        out_specs=pl.BlockSpec((n, h * (kl + LANE)), lambda i: (0, 0)),
        compiler_params=_cp("arbitrary"),
        name="absorb",
    )(q, gkh_pad, wukt, sel)


def _paged_kernel(pt_ref, a_ref, cn_ref, pn_ref, wuk_ref, ones_ref, onesr_ref, wuv_ref, *rest, pg, kl, rope, qk, h,
                  vd, t):
    c_refs = rest[:pg]
    p_refs = rest[pg:2 * pg]
    o_ref, m_sc, l_sc, acc_sc = rest[2 * pg:]
    del pt_ref
    j = pl.program_id(1)
    th = t * h
    a = a_ref[0]
    a_c = a[:, :kl].astype(BF16)
    a_p = a[:, kl:kl + rope].astype(BF16)
    inv_qk = 1.0 / qk

    @pl.when(j == 0)
    def _():
        m_sc[...] = jnp.full(m_sc.shape, NEG, F32)
        l_sc[...] = jnp.zeros(l_sc.shape, F32)
        acc_sc[...] = jnp.zeros(acc_sc.shape, F32)

    eye = (lax.broadcasted_iota(I32, (th, th), 0) == lax.broadcasted_iota(I32, (th, th), 1))

    def update(cf, pf, valid):
        cb = cf.astype(BF16)
        pb = pf.astype(BF16)
        kn = _dot(cb, wuk_ref[...])
        hs = _dot((kn * kn).astype(BF16), ones_ref[...]) + _dot((pf * pf).astype(BF16), onesr_ref[...])
        num = lax.dot_general(cb, a_c, (((1,), (1,)), ((), ())), preferred_element_type=F32)
        num = num + lax.dot_general(pb, a_p, (((1,), (1,)), ((), ())), preferred_element_type=F32)
        s = num * lax.rsqrt(hs * inv_qk + EPS)
        if valid is not None:
            s = jnp.where(valid, s, NEG)
        m_old = m_sc[0:1, 0:th]
        m_new = jnp.maximum(m_old, jnp.max(s, axis=0, keepdims=True))
        alpha = jnp.exp(m_old - m_new)
        p = jnp.exp(s - m_new)
        l_sc[0:1, 0:th] = alpha * l_sc[0:1, 0:th] + jnp.sum(p, axis=0, keepdims=True)
        m_sc[0:1, 0:th] = m_new
        alpha_col = jnp.sum(jnp.where(eye, alpha, 0.0), axis=1, keepdims=True)
        pv = lax.dot_general(p.astype(BF16), cb, (((0,), (0,)), ((), ())), preferred_element_type=F32)
        acc_sc[...] = alpha_col * acc_sc[...] + pv

    cf = jnp.concatenate([r[0] for r in c_refs], axis=0)
    pf = jnp.concatenate([r[0] for r in p_refs], axis=0)
    update(cf, pf, None)

    @pl.when(j == pl.num_programs(1) - 1)
    def _():
        tn = cn_ref.shape[1]
        krow = lax.broadcasted_iota(I32, (tn, th), 0)
        qcol = lax.broadcasted_iota(I32, (tn, th), 1) // h
        update(cn_ref[0], pn_ref[0], jnp.logical_and(krow <= qcol, krow < t))
        l_col = jnp.sum(jnp.where(eye, l_sc[0:1, 0:th], 0.0), axis=1, keepdims=True)
        on = (acc_sc[...] / l_col).astype(BF16)
        oa = _dot(on, wuv_ref[...])
        r_h = lax.broadcasted_iota(I32, (th, h * vd), 0) % h
        c_h = lax.broadcasted_iota(I32, (th, h * vd), 1) // vd
        oa = jnp.where(r_h == c_h, oa, 0.0)
        rows = [jnp.sum(oa[tt * h:(tt + 1) * h, :], axis=0, keepdims=True) for tt in range(t)]
        o_ref[0] = jnp.concatenate(rows, axis=0)


def _paged_attention(a, c_new, pe_new, pool_c, pool_pe, page_table, wp, dims):
    bs, th, aw = a.shape
    t = c_new.shape[1]
    tn = -(-t // (2 * SUBLANE)) * (2 * SUBLANE)
    c_new = jnp.pad(c_new, [(0, 0), (0, tn - t), (0, 0)])
    pe_new = jnp.pad(pe_new, [(0, 0), (0, tn - t), (0, 0)])
    kl, rope, qk, h, vd, nope = dims["kl"], dims["rope"], dims["qk"], dims["h"], dims["vd"], dims["nope"]
    n_pages = page_table.shape[1]
    ps = pool_c.shape[1]
    pg = _tile(n_pages, 8, 1)
    pt = page_table.reshape(-1).astype(I32)

    def page_map(i, j, ptr, *, u):
        return (ptr[i * n_pages + j * pg + u], 0, 0)

    cst = lambda arr: pl.BlockSpec(arr.shape, lambda i, j, ptr: (0,) * arr.ndim)
    in_specs = [
        pl.BlockSpec((1, th, aw), lambda i, j, ptr: (i, 0, 0)),
        pl.BlockSpec((1, tn, kl), lambda i, j, ptr: (i, 0, 0)),
        pl.BlockSpec((1, tn, rope), lambda i, j, ptr: (i, 0, 0)),
        cst(wp["wuk_flat"]), cst(wp["ones_th"]), cst(wp["ones_r"]), cst(wp["wuv"]),
    ]
    in_specs += [pl.BlockSpec((1, ps, kl), functools.partial(page_map, u=u)) for u in range(pg)]
    in_specs += [pl.BlockSpec((1, ps, rope), functools.partial(page_map, u=u)) for u in range(pg)]
    grid_spec = pltpu.PrefetchScalarGridSpec(
        num_scalar_prefetch=1,
        grid=(bs, n_pages // pg),
        in_specs=in_specs,
        out_specs=pl.BlockSpec((1, t, h * vd), lambda i, j, ptr: (i, 0, 0)),
        scratch_shapes=[pltpu.VMEM((SUBLANE, LANE), F32), pltpu.VMEM((SUBLANE, LANE), F32),
                        pltpu.VMEM((th, kl), F32)],
    )
    return pl.pallas_call(
        functools.partial(_paged_kernel, pg=pg, kl=kl, rope=rope, qk=qk, h=h, vd=vd, t=t),
        out_shape=jax.ShapeDtypeStruct((bs, t, h * vd), F32),
        grid_spec=grid_spec,
        compiler_params=_cp("parallel", "arbitrary"),
        name="paged_attention",
    )(pt, a, c_new, pe_new, wp["wuk_flat"], wp["ones_th"], wp["ones_r"], wp["wuv"],
      *([pool_c] * pg), *([pool_pe] * pg))


def _mixer_kernel(x_ref, yc_ref, at_ref, sh1_ref, sc1_ref, g1_ref, sh2_ref, sc2_ref, gmix_ref, wg_ref, wco_ref,
                  wao_ref, wout_ref, gffn_ref, wrh_ref, wrl_ref, br_ref,
                  xm_ref, h2_ref, idx_ref, prob_ref, *, d, e):
    x = x_ref[...]
    hmod = _rms(x, gmix_ref[...]) * (1.0 + sc1_ref[0]) + sh1_ref[0]
    gates = _dot(hmod.astype(BF16), wg_ref[...])
    conv_out = _dot(yc_ref[...].astype(BF16), wco_ref[...])
    attn_out = _dot(at_ref[...].astype(BF16), wao_ref[...])
    merged = _sigmoid(gates[:, :d]) * conv_out + _sigmoid(gates[:, d:]) * attn_out
    xm = x + g1_ref[0] * _dot(merged.astype(BF16), wout_ref[...])
    xm_ref[...] = xm
    h2 = _rms(xm, gffn_ref[...]) * (1.0 + sc2_ref[0]) + sh2_ref[0]
    h2_ref[...] = h2
    hi = h2.astype(BF16)
    lo = (h2 - hi.astype(F32)).astype(BF16)
    lg = _dot(hi, wrh_ref[...]) + _dot(lo, wrh_ref[...]) + _dot(hi, wrl_ref[...]) + br_ref[...]
    tm = x.shape[0]
    lane = lax.broadcasted_iota(I32, (tm, LANE), 1)
    lane_f = lane.astype(F32)
    lg = jnp.where(lane < e, lg, -jnp.inf)
    vals, idxs = [], []
    for _ in range(TOP_K):
        m = jnp.max(lg, axis=-1, keepdims=True)
        ik = jnp.min(jnp.where(lg == m, lane_f, float(LANE)), axis=-1, keepdims=True)
        vals.append(m)
        idxs.append(ik)
        lg = jnp.where(lane_f == ik, -jnp.inf, lg)
    ex = [jnp.exp(vk - vals[0]) for vk in vals]
    den = ex[0]
    for ek in ex[1:]:
        den = den + ek
    idx_o = jnp.zeros((tm, LANE), F32)
    prob_o = jnp.zeros((tm, LANE), F32)
    for k in range(TOP_K):
        idx_o = jnp.where(lane == k, idxs[k], idx_o)
        prob_o = jnp.where(lane == k, ex[k] / den, prob_o)
    idx_ref[...] = idx_o.astype(I32)
    prob_ref[...] = prob_o


def _mixer(x, yc, at, mods, mod_map, wp, dims, tm):
    n, d = x.shape
    e = dims["e"]
    mblk = (1,) + mods[0].shape[1:]
    full = lambda a: pl.BlockSpec(a.shape, lambda i: (0,) * a.ndim)
    row = lambda w: pl.BlockSpec((tm, w), lambda i: (i, 0))
    mspec = pl.BlockSpec(mblk, mod_map)
    weights = [wp["gmix"], wp["wg"], wp["wco"], wp["wao"], wp["wout"], wp["gffn"], wp["wrh"], wp["wrl"], wp["br"]]
    return pl.pallas_call(
        functools.partial(_mixer_kernel, d=d, e=e),
        out_shape=(
            jax.ShapeDtypeStruct((n, d), F32),
            jax.ShapeDtypeStruct((n, d), F32),
            jax.ShapeDtypeStruct((n, LANE), I32),
            jax.ShapeDtypeStruct((n, LANE), F32),
        ),
        grid=(n // tm,),
        in_specs=[row(d), row(yc.shape[1]), row(at.shape[1]), mspec, mspec, mspec, mspec, mspec]
        + [full(w) for w in weights],
        out_specs=(row(d), row(d), row(LANE), row(LANE)),
        compiler_params=_cp("parallel"),
        name="mixer",
    )(x, yc, at, mods[0], mods[1], mods[2], mods[3], mods[4], *weights)


def _lane_cumsum(x):
    lane = lax.broadcasted_iota(I32, x.shape, 1)
    s = 1
    while s < LANE:
        x = x + jnp.where(lane >= s, pltpu.roll(x, s, axis=1), 0.0)
        s *= 2
    return x


def _route_kernel(idx_ref, dest_ref, meta_ref, run_ref, tot_ref, *, e, blk, nbp, tm):
    ph = pl.program_id(0)
    i = pl.program_id(1)
    lane = lax.broadcasted_iota(I32, (tm, LANE), 1)
    idx = idx_ref[...]
    oh = [lane == idx[:, k:k + 1] for k in range(TOP_K)]
    sel = jnp.zeros((tm, LANE), F32)
    for k in range(TOP_K):
        sel = sel + oh[k].astype(F32)
    tile_cnt = jnp.sum(sel, axis=0, keepdims=True)

    @pl.when(jnp.logical_and(ph == 0, i == 0))
    def _():
        run_ref[...] = jnp.zeros(run_ref.shape, F32)

    @pl.when(ph == 0)
    def _():
        run_ref[...] = run_ref[...] + tile_cnt

    @pl.when(jnp.logical_and(ph == 1, i == 0))
    def _():
        cnt = run_ref[...]
        padded = jnp.ceil(cnt * (1.0 / blk)) * blk
        pend = _lane_cumsum(padded)
        pstart = pend - padded
        tot_ref[0] = pstart
        tot_ref[1] = pend
        tot_ref[2] = cnt
        run_ref[...] = jnp.zeros(run_ref.shape, F32)
        lane_b = lax.broadcasted_iota(I32, (nbp, LANE), 1)
        start = (lax.broadcasted_iota(I32, (nbp, LANE), 0) * blk).astype(F32)
        pend_r = pend[0:1, :]
        be = jnp.sum(jnp.where(jnp.logical_and(lane_b < e, pend_r <= start), 1.0, 0.0), axis=-1, keepdims=True)
        be = jnp.minimum(be, float(e - 1))
        ohb = lane_b.astype(F32) == be
        cnt_b = jnp.sum(jnp.where(ohb, cnt[0:1, :], 0.0), axis=-1, keepdims=True)
        pst_b = jnp.sum(jnp.where(ohb, pstart[0:1, :], 0.0), axis=-1, keepdims=True)
        nvalid = jnp.clip(cnt_b - (start[:, 0:1] - pst_b), 0.0, float(blk))
        nvb = jnp.max(jnp.where(lane_b == e - 1, pend_r, 0.0), axis=-1, keepdims=True) * (1.0 / blk)
        meta = jnp.where(lane_b == 0, be, jnp.where(lane_b == 1, nvalid, jnp.where(lane_b == 2, nvb, 0.0)))
        meta_ref[...] = meta.astype(I32)

    @pl.when(ph == 1)
    def _():
        r = lax.broadcasted_iota(I32, (tm, tm), 0)
        c = lax.broadcasted_iota(I32, (tm, tm), 1)
        tri = jnp.where(c < r, 1.0, 0.0).astype(BF16)
        prefix = _dot(tri, sel.astype(BF16))
        base = prefix + run_ref[0:1, :] + tot_ref[0, 0:1, :]
        dest = jnp.zeros((tm, LANE), F32)
        for k in range(TOP_K):
            dk = jnp.sum(jnp.where(oh[k], base, 0.0), axis=-1, keepdims=True)
            dest = jnp.where(lane == k, dk, dest)
        dest_ref[...] = dest.astype(I32)
        run_ref[...] = run_ref[...] + tile_cnt


def _route(idx, e, blk, nbp):
    n = idx.shape[0]
    tm = _tile(n, 256)
    assert tm <= 256
    return pl.pallas_call(
        functools.partial(_route_kernel, e=e, blk=blk, nbp=nbp, tm=tm),
        out_shape=(jax.ShapeDtypeStruct((n, LANE), I32), jax.ShapeDtypeStruct((nbp, LANE), I32)),
        grid=(2, n // tm),
        in_specs=[pl.BlockSpec((tm, LANE), lambda ph, i: (i, 0))],
        out_specs=(pl.BlockSpec((tm, LANE), lambda ph, i: (i * ph, 0)),
                   pl.BlockSpec((nbp, LANE), lambda ph, i: (0, 0))),
        scratch_shapes=[pltpu.VMEM((SUBLANE, LANE), F32), pltpu.VMEM((3, SUBLANE, LANE), F32)],
        compiler_params=_cp("arbitrary", "arbitrary"),
        name="route",
    )(idx)


def _dispatch_kernel(dest_ref, h_ref, buf_ref, sem, *, tm):
    def body(t, carry):
        for k in range(TOP_K):
            dst = dest_ref[t * TOP_K + k]
            pltpu.make_async_copy(h_ref.at[pl.ds(t, 1), :], buf_ref.at[pl.ds(dst, 1), :], sem).start()
        return carry

    lax.fori_loop(0, tm, body, 0)
    for k in range(TOP_K):
        pltpu.make_async_copy(h_ref, buf_ref.at[pl.ds(0, tm), :], sem).wait()


def _dispatch(h2, dest_flat, rows):
    n, d = h2.shape
    tm = 256 if n % 256 == 0 else n
    return pl.pallas_call(
        functools.partial(_dispatch_kernel, tm=tm),
        out_shape=jax.ShapeDtypeStruct((rows, d), h2.dtype),
        grid=(n // tm,),
        in_specs=[pl.BlockSpec((tm * TOP_K,), lambda i: (i,), memory_space=pltpu.SMEM),
                  pl.BlockSpec((tm, d), lambda i: (i, 0))],
        out_specs=pl.BlockSpec(memory_space=pl.ANY),
        scratch_shapes=[pltpu.SemaphoreType.DMA(())],
        compiler_params=_cp("arbitrary"),
        name="dispatch",
    )(dest_flat, h2)


def _expert_kernel(be_ref, nv_ref, nb_ref, x_ref, wup_ref, bup_ref, wdn_ref, bdn_ref, y_ref, *, blk, f):
    b = pl.program_id(0)
    del be_ref

    @pl.when(b < nb_ref[0])
    def _():
        rowi = lax.broadcasted_iota(I32, (blk, 1), 0)
        x = jnp.where(rowi < nv_ref[b], x_ref[...], 0.0).astype(BF16)
        gu = _dot(x, wup_ref[0]) + bup_ref[0]
        gate = jnp.minimum(gu[:, :f], SWIGLU_LIMIT)
        lin = jnp.clip(gu[:, f:], -SWIGLU_LIMIT, SWIGLU_LIMIT)
        act = gate * _sigmoid(SWIGLU_ALPHA * gate) * (lin + 1.0)
        y_ref[...] = _dot(act.astype(BF16), wdn_ref[0]) + bdn_ref[0]


def _experts(buf, be, nv, nb, w_up, b_up, w_dn, b_dn, blk):
    rows, d = buf.shape
    e, _, f2 = w_up.shape
    f = f2 // 2
    nblk = rows // blk

    def blk_map(b, be_r, nv_r, nb_r):
        return (jnp.minimum(b, nb_r[0] - 1), 0)

    def exp_map(b, be_r, nv_r, nb_r):
        return (be_r[jnp.minimum(b, nb_r[0] - 1)], 0, 0)

    grid_spec = pltpu.PrefetchScalarGridSpec(
        num_scalar_prefetch=3,
        grid=(nblk,),
        in_specs=[
            pl.BlockSpec((blk, d), blk_map),
            pl.BlockSpec((1, d, f2), exp_map),
            pl.BlockSpec((1, 1, f2), exp_map),
            pl.BlockSpec((1, f, d), exp_map),
            pl.BlockSpec((1, 1, d), exp_map),
        ],
        out_specs=pl.BlockSpec((blk, d), blk_map),
    )
    return pl.pallas_call(
        functools.partial(_expert_kernel, blk=blk, f=f),
        out_shape=jax.ShapeDtypeStruct((rows, d), F32),
        grid_spec=grid_spec,
        compiler_params=_cp("arbitrary"),
        name="experts",
    )(be, nv, nb, buf, w_up, b_up.reshape(e, 1, f2), w_dn, b_dn.reshape(e, 1, d))


def _combine_kernel(dcur_ref, dnxt_ref, x_ref, g2_ref, p_ref, y_ref, o_ref, yg_ref, sem, *, tm):
    i = pl.program_id(0)
    n = pl.num_programs(0)
    slot = i % 2

    def issue(dref, s):
        def body(t, carry):
            for k in range(TOP_K):
                src = dref[t * TOP_K + k]
                pltpu.make_async_copy(y_ref.at[pl.ds(src, 1), :], yg_ref.at[s, k, pl.ds(t, 1), :], sem.at[s]).start()
            return carry

        lax.fori_loop(0, tm, body, 0)

    @pl.when(i == 0)
    def _():
        issue(dcur_ref, 0)

    @pl.when(i + 1 < n)
    def _():
        issue(dnxt_ref, 1 - slot)

    for k in range(TOP_K):
        pltpu.make_async_copy(y_ref.at[pl.ds(0, tm), :], yg_ref.at[slot, k], sem.at[slot]).wait()
    prob = p_ref[...]
    acc = prob[:, 0:1] * yg_ref[slot, 0]
    for k in range(1, TOP_K):
        acc = acc + prob[:, k:k + 1] * yg_ref[slot, k]
    o_ref[...] = x_ref[...] + g2_ref[0] * acc


def _combine(xm, g2, mod_map, prob, ybuf, dest_flat):
    n, d = xm.shape
    tm = 256 if n % 256 == 0 else n
    nt = n // tm
    mblk = (1,) + g2.shape[1:]
    return pl.pallas_call(
        functools.partial(_combine_kernel, tm=tm),
        out_shape=jax.ShapeDtypeStruct((n, d), F32),
        grid=(nt,),
        in_specs=[
            pl.BlockSpec((tm * TOP_K,), lambda i: (i,), memory_space=pltpu.SMEM),
            pl.BlockSpec((tm * TOP_K,), lambda i: (jnp.minimum(i + 1, nt - 1),), memory_space=pltpu.SMEM),
            pl.BlockSpec((tm, d), lambda i: (i, 0)),
            pl.BlockSpec(mblk, mod_map),
            pl.BlockSpec((tm, LANE), lambda i: (i, 0)),
            pl.BlockSpec(memory_space=pl.ANY),
        ],
        out_specs=pl.BlockSpec((tm, d), lambda i: (i, 0)),
        scratch_shapes=[pltpu.VMEM((2, TOP_K, tm, d), F32), pltpu.SemaphoreType.DMA((2,))],
        compiler_params=_cp("arbitrary"),
        name="combine",
    )(dest_flat, dest_flat, xm, g2, prob, ybuf)


def _rope_tables(pos, nope, rope, qk):
    p = pos.shape[0]
    inv_freq = ROPE_THETA ** (-jnp.arange(0, rope, 2, dtype=F32) / rope)
    ang = pos.astype(F32)[:, None] * inv_freq
    c2 = jnp.concatenate([jnp.cos(ang)] * 2, axis=-1)
    s2 = jnp.concatenate([jnp.sin(ang)] * 2, axis=-1)
    cq = jnp.concatenate([jnp.ones((p, nope), F32), c2, jnp.ones((p, LANE - qk), F32)], axis=-1)
    sq = jnp.concatenate([jnp.zeros((p, nope), F32), s2, jnp.zeros((p, LANE - qk), F32)], axis=-1)
    ck = jnp.concatenate([c2, jnp.zeros((p, LANE - rope), F32)], axis=-1)
    sk = jnp.concatenate([s2, jnp.zeros((p, LANE - rope), F32)], axis=-1)
    return cq, sq, ck, sk


def _swap_halves(w):
    half = w.shape[-1] // 2
    return jnp.concatenate([-w[..., half:], w[..., :half]], axis=-1)


def _pad_last(w, n):
    return jnp.pad(w, [(0, 0)] * (w.ndim - 1) + [(0, n - w.shape[-1])])


def _prepare(p, dims, t_s):
    (g_mix, g_ffn, w_in, g_ql, w_uq, g_kvl, w_uk, w_uv, g_qh, g_kh, w_ao, w_out, w_co, w_r, b_r) = p
    d, c, ql, kl, h = dims["d"], dims["c"], dims["ql"], dims["kl"], dims["h"]
    nope, rope, qk, vd, e = dims["nope"], dims["rope"], dims["qk"], dims["vd"], dims["e"]
    o_q = 2 * c
    o_kv = o_q + ql
    o_kr = o_kv + kl
    o_ga = o_kr + rope
    w_kr = w_in[:, o_kr:o_ga]
    w1 = jnp.concatenate([w_in[:, :o_kr], _pad_last(w_kr, LANE), _pad_last(_swap_halves(w_kr), LANE)], axis=1)
    wq_plain = _pad_last(w_uq, LANE)
    wq_swap = jnp.concatenate([jnp.zeros((ql, h, nope), F32), _swap_halves(w_uq[..., nope:]),
                               jnp.zeros((ql, h, LANE - qk), F32)], axis=-1)
    wq = jnp.concatenate([wq_plain.reshape(ql, h * LANE), wq_swap.reshape(ql, h * LANE)], axis=1)
    sm_scale = qk ** -0.5
    place = jnp.zeros((LANE, LANE), F32).at[jnp.arange(rope), nope + jnp.arange(rope)].set(1.0)
    epe = jnp.tile(place, (1, h))
    wukt = _pad_last(jnp.transpose(w_uk, (1, 2, 0)), kl)
    wukt = jnp.pad(wukt, [(0, 0), (0, LANE - nope), (0, 0)])
    sel = jnp.zeros((LANE, LANE), F32).at[nope + jnp.arange(rope), jnp.arange(rope)].set(1.0)
    th = t_s * h
    head_of_row = jnp.arange(h * nope) // nope
    ones_th = (head_of_row[:, None] == (jnp.arange(th) % h)[None, :]).astype(BF16)
    hi = w_r.astype(BF16)
    lo = (w_r - hi.astype(F32)).astype(BF16)
    row = lambda g: g.reshape(1, -1).astype(F32)
    return dict(
        gmix=row(g_mix), gffn=row(g_ffn), w1=w1.astype(BF16), gql=row(g_ql), wq=wq.astype(BF16),
        gqh=row(_pad_last(g_qh, LANE)) * sm_scale, gkvl=row(g_kvl),
        wuk=_pad_last(w_uk, LANE).reshape(kl, h * LANE).astype(BF16), epe=epe.astype(BF16),
        gkh=row(_pad_last(g_kh, LANE)), wuv=w_uv.reshape(kl, h * vd).astype(BF16),
        wukt=wukt.astype(BF16), sel=sel.astype(BF16), wuk_flat=w_uk.reshape(kl, h * nope).astype(BF16),
        ones_th=ones_th, ones_r=jnp.ones((rope, th), BF16),
        wg=w_in[:, o_ga:].astype(BF16), wco=w_co.astype(BF16), wao=w_ao.astype(BF16), wout=w_out.astype(BF16),
        wrh=_pad_last(hi, LANE), wrl=_pad_last(lo, LANE), br=row(_pad_last(b_r, LANE)),
    )


def _moe(xm, h2, idx, prob, g2, mod_map, w_up, b_up, w_dn, b_dn, e):
    n = xm.shape[0]
    n_asg = n * TOP_K
    blk = 512 if n_asg >= 2048 * e else 128
    nblk = -(-n_asg // blk) + e
    nbp = -(-nblk // SUBLANE) * SUBLANE
    dest, meta = _route(idx, e, blk, nbp)
    dest_flat = dest[:, :TOP_K].reshape(-1)
    buf = _dispatch(h2, dest_flat, nblk * blk)
    ybuf = _experts(buf, meta[:, 0], meta[:, 1], meta[:1, 2], w_up, b_up, w_dn, b_dn, blk)
    return _combine(xm, g2, mod_map, prob, ybuf, dest_flat)


def _layer(x, mod, pos_tabs, conv_state, attend, wp, ex, dims, group):
    n, d = x.shape
    tm = _tile(n, 512)
    if mod[0].shape[1] == 1:
        per = group // tm
        mod_map = lambda i: (i // per, 0, 0)
        tab_map = lambda i: (i % per, 0)
    else:
        mod_map = lambda i: (i, 0, 0)
        tab_map = lambda i: (i, 0)
    shift1, scale1, gate1, shift2, scale2, gate2 = mod
    v, q, ckv, kpe, k, vv = _inproj(x, (shift1, scale1), mod_map, pos_tabs, tab_map, wp, dims, tm)
    yc, new_state = conv_state(v)
    at = attend(q, ckv, kpe, k, vv)
    xm, h2, idx, prob = _mixer(x, yc, at, (shift1, scale1, gate1, shift2, scale2), mod_map, wp, dims, tm)
    if mod[0].shape[1] == 1:
        tmc = 256 if n % 256 == 0 else n
        perc = group // tmc
        comb_map = lambda i: (i // perc, 0, 0)
        g2 = gate2
    else:
        tmc = 256 if n % 256 == 0 else n
        comb_map = lambda i: (i, 0, 0)
        g2 = gate2.reshape(n // tmc, tmc, d)
    y = _moe(xm, h2, idx, prob, g2, comb_map, *ex, dims["e"])
    return y, ckv, kpe, new_state


def kernel(x_prompt, x_sample, cache_kv_latent, cache_k_rope, state_conv, page_table, c_prompt, c_sample, w_ada, b_ada, g_norm_mix, g_norm_ffn, w_in, w_dw, b_dw, g_conv_ln, b_conv_ln, w_conv_out, g_q_lat, w_uq, g_kv_lat, w_uk, w_uv, g_q_head, g_k_head, w_attn_out, w_out, w_router, b_router, w_up, b_up, w_down, b_down):
    b, s, d = x_prompt.shape
    bs, ts, _ = x_sample.shape
    depth = w_ada.shape[0]
    width, c = w_dw.shape[1], w_dw.shape[2]
    ql, h, qk = w_uq.shape[1], w_uq.shape[2], w_uq.shape[3]
    kl, nope, vd = w_uk.shape[1], w_uk.shape[3], w_uv.shape[3]
    rope = qk - nope
    e = w_router.shape[2]
    n_pages, ps = page_table.shape[1], cache_kv_latent.shape[2]
    past_len = n_pages * ps
    dims = dict(d=d, c=c, ql=ql, kl=kl, h=h, nope=nope, rope=rope, qk=qk, vd=vd, e=e)
    assert qk <= LANE and rope % 2 == 0 and e <= LANE and s >= width - 1
    assert all(v % LANE == 0 for v in (d, c, ql, kl)) and ts * h <= LANE

    n_p, n_s = b * s, bs * ts
    tabs_p = _rope_tables(jnp.arange(s, dtype=I32), nope, rope, qk)
    tabs_s = _rope_tables(jnp.tile(past_len + jnp.arange(ts, dtype=I32), bs), nope, rope, qk)
    tm_s = _tile(n_s, 512)
    ts_pad = -(-ts // (2 * SUBLANE)) * (2 * SUBLANE)

    yp = x_prompt.reshape(n_p, d)
    ys = x_sample.reshape(n_s, d)
    outs = [[] for _ in range(6)]
    for l in range(depth):
        wp = _prepare((g_norm_mix[l], g_norm_ffn[l], w_in[l], g_q_lat[l], w_uq[l], g_kv_lat[l], w_uk[l], w_uv[l],
                       g_q_head[l], g_k_head[l], w_attn_out[l], w_out[l], w_conv_out[l], w_router[l], b_router[l]),
                      dims, ts)
        ex = (w_up[l].astype(BF16), b_up[l], w_down[l].astype(BF16), b_down[l])
        mod = _adaln(jnp.concatenate([c_prompt, c_sample], axis=0), w_ada[l], b_ada[l]).reshape(b + bs, N_MOD, d)
        mod_p = [mod[:b, j].reshape(b, 1, d) for j in range(N_MOD)]
        mod_s = [jnp.repeat(mod[b:, j], ts, axis=0).reshape(n_s // tm_s, tm_s, d) for j in range(N_MOD)]

        def conv_p(v):
            yc, ns = _conv(v.reshape(b, s, c), jnp.zeros((b, width - 1, c), F32), w_dw[l], b_dw[l], g_conv_ln[l],
                           b_conv_ln[l], s)
            return yc.reshape(n_p, c), ns

        def conv_s(v):
            vp = jnp.pad(v.reshape(bs, ts, c), [(0, 0), (0, ts_pad - ts), (0, 0)])
            yc, ns = _conv(vp, state_conv[l], w_dw[l], b_dw[l], g_conv_ln[l], b_conv_ln[l], ts)
            return yc[:, :ts].reshape(n_s, c), ns

        def attend_p(q, ckv, kpe, k, vv):
            return _attention(q, k, vv, b, s, h, vd)

        def attend_s(q, ckv, kpe, k, vv):
            a = _absorb(q, wp["gkh"], wp["wukt"], wp["sel"], h, kl).reshape(bs, ts * h, kl + LANE)
            o = _paged_attention(a, ckv.reshape(bs, ts, kl), kpe.reshape(bs, ts, rope), cache_kv_latent[l],
                                 cache_k_rope[l], page_table, wp, dims)
            return o.reshape(n_s, h * vd)

        yp, c_p, r_p, s_p = _layer(yp, mod_p, tabs_p, conv_p, attend_p, wp, ex, dims, s)
        ys, c_s, r_s, s_s = _layer(ys, mod_s, tabs_s, conv_s, attend_s, wp, ex, dims, tm_s)
        for lst, val in zip(outs, (c_p.reshape(b, s, kl), r_p.reshape(b, s, rope), s_p, c_s.reshape(bs, ts, kl),
                                   r_s.reshape(bs, ts, rope), s_s)):
            lst.append(val)
    return (yp.reshape(b, s, d), ys.reshape(bs, ts, d)) + tuple(jnp.stack(o) for o in outs)
```

```python
import functools

import jax
import jax.numpy as jnp
from jax import lax
from jax.experimental import pallas as pl
from jax.experimental.pallas import tpu as pltpu

F32 = jnp.float32
BF16 = jnp.bfloat16
I32 = jnp.int32
U32 = jnp.uint32

EPS = 1e-6
TOP_K = 4
ROPE_THETA = 10000.0
SWIGLU_LIMIT = 7.0
SWIGLU_ALPHA = 1.702
N_MOD = 6

LANE = 128
SUBLANE = 8
VMEM_LIMIT_BYTES = 56 * 1024 * 1024
NEG = -1e30


def _cp(*sem):
    return pltpu.CompilerParams(dimension_semantics=sem, vmem_limit_bytes=VMEM_LIMIT_BYTES)


def _tile(n, pref, mult=SUBLANE):
    t = min(pref, n)
    t -= t % mult
    while t >= mult:
        if n % t == 0:
            return t
        t -= mult
    return n


def _sigmoid(x):
    return 1.0 / (1.0 + jnp.exp(-x))


def _rms(x, g):
    return x * lax.rsqrt(jnp.mean(x * x, axis=-1, keepdims=True) + EPS) * g


def _dot(a, b):
    return jnp.dot(a, b, preferred_element_type=F32)


def _pack_rows(x):
    half = x.shape[1] // 2
    hi = lax.bitcast_convert_type(x[:, :half].astype(BF16).astype(F32), U32)
    lo = lax.bitcast_convert_type(x[:, half:].astype(BF16).astype(F32), U32)
    return hi | (lo >> 16)


def _unpack_rows(u):
    return (lax.bitcast_convert_type(u & jnp.uint32(0xFFFF0000), F32),
            lax.bitcast_convert_type(u << 16, F32))


def _adaln_kernel(c_ref, w_ref, b_ref, o_ref):
    c = c_ref[...]
    s = (c * _sigmoid(c)).astype(BF16)
    o_ref[...] = _dot(s, w_ref[...].astype(BF16)) + b_ref[...]


def _adaln(c, w_ada, b_ada):
    r, d = c.shape
    n = w_ada.shape[1]
    tn = _tile(n, 1536, LANE)
    return pl.pallas_call(
        _adaln_kernel,
        out_shape=jax.ShapeDtypeStruct((r, n), F32),
        grid=(n // tn,),
        in_specs=[
            pl.BlockSpec((r, d), lambda j: (0, 0)),
            pl.BlockSpec((d, tn), lambda j: (0, j)),
            pl.BlockSpec((1, tn), lambda j: (0, j)),
        ],
        out_specs=pl.BlockSpec((r, tn), lambda j: (0, j)),
        compiler_params=_cp("arbitrary"),
        name="adaln",
    )(c, w_ada, b_ada.reshape(1, n))


def _inproj_kernel(x_ref, sh_ref, sc_ref, cq_ref, sq_ref, ck_ref, sk_ref, gmix_ref, w1_ref, gql_ref, wq_ref,
                   gqh_ref, gkvl_ref, wuk_ref, epe_ref, gkh_ref, wuv_ref,
                   v_ref, q_ref, ckv_ref, kpe_ref, k_ref, vv_ref, *, c, ql, kl, h, qk, rope):
    x = x_ref[...]
    hmod = _rms(x, gmix_ref[...]) * (1.0 + sc_ref[0]) + sh_ref[0]
    p = _dot(hmod.astype(BF16), w1_ref[...])
    v_ref[...] = p[:, :c] * _sigmoid(p[:, c:2 * c])
    o_q = 2 * c
    o_kv = o_q + ql
    o_kr = o_kv + kl
    qn = _rms(p[:, o_q:o_kv], gql_ref[...])
    qq = _dot(qn.astype(BF16), wq_ref[...])
    cq = cq_ref[...]
    sq = sq_ref[...]
    inv_qk = 1.0 / qk
    for hd in range(h):
        qr = qq[:, hd * LANE:(hd + 1) * LANE] * cq + qq[:, (h + hd) * LANE:(h + hd + 1) * LANE] * sq
        ms = jnp.sum(qr * qr, axis=-1, keepdims=True) * inv_qk
        q_ref[:, hd * LANE:(hd + 1) * LANE] = (qr * lax.rsqrt(ms + EPS) * gqh_ref[...]).astype(q_ref.dtype)
    ckv = _rms(p[:, o_kv:o_kr], gkvl_ref[...])
    ckv_ref[...] = ckv
    kpe = p[:, o_kr:o_kr + LANE] * ck_ref[...] + p[:, o_kr + LANE:o_kr + 2 * LANE] * sk_ref[...]
    kpe_ref[...] = kpe[:, :rope]
    cb = ckv.astype(BF16)
    kf = _dot(cb, wuk_ref[...]) + _dot(kpe.astype(BF16), epe_ref[...])
    for hd in range(h):
        kh = kf[:, hd * LANE:(hd + 1) * LANE]
        ms = jnp.sum(kh * kh, axis=-1, keepdims=True) * inv_qk
        k_ref[:, hd * LANE:(hd + 1) * LANE] = (kh * lax.rsqrt(ms + EPS) * gkh_ref[...]).astype(k_ref.dtype)
    vv_ref[...] = _dot(cb, wuv_ref[...]).astype(vv_ref.dtype)


def _inproj(x, mods, mod_map, tabs, tab_map, wp, dims, tm):
    n, d = x.shape
    c, ql, kl, h, qk, rope, vd = dims["c"], dims["ql"], dims["kl"], dims["h"], dims["qk"], dims["rope"], dims["vd"]
    mblk = (1,) + mods[0].shape[1:]
    full = lambda a: pl.BlockSpec(a.shape, lambda i: (0,) * a.ndim)
    row = lambda w: pl.BlockSpec((tm, w), lambda i: (i, 0))
    tab = pl.BlockSpec((tm, LANE), tab_map)
    weights = [wp["gmix"], wp["w1"], wp["gql"], wp["wq"], wp["gqh"], wp["gkvl"], wp["wuk"], wp["epe"], wp["gkh"],
               wp["wuv"]]
    return pl.pallas_call(
        functools.partial(_inproj_kernel, c=c, ql=ql, kl=kl, h=h, qk=qk, rope=rope),
        out_shape=(
            jax.ShapeDtypeStruct((n, c), F32),
            jax.ShapeDtypeStruct((n, h * LANE), BF16),
            jax.ShapeDtypeStruct((n, kl), F32),
            jax.ShapeDtypeStruct((n, rope), F32),
            jax.ShapeDtypeStruct((n, h * LANE), BF16),
            jax.ShapeDtypeStruct((n, h * vd), BF16),
        ),
        grid=(n // tm,),
        in_specs=[row(d), pl.BlockSpec(mblk, mod_map), pl.BlockSpec(mblk, mod_map), tab, tab, tab, tab]
        + [full(w) for w in weights],
        out_specs=(row(c), row(h * LANE), row(kl), row(rope), row(h * LANE), row(h * vd)),
        compiler_params=_cp("parallel"),
        name="inproj",
    )(x, mods[0], mods[1], *tabs, *weights)


def _conv_kernel(v_ref, st_ref, wdw_ref, gln_ref, bln_ref, y_ref, ns_ref, win_ref, xs_ref, pre_ref, *, width, tt,
                 tv, rb, halo):
    j = pl.program_id(1)
    off = halo - (width - 1)
    nl = win_ref.shape[0]
    for lt in range(nl):
        ls = slice(lt * LANE, (lt + 1) * LANE)

        @pl.when(j == 0)
        def _():
            win_ref[lt, pl.ds(off, width - 1), :] = st_ref[0, :, ls]

        @pl.when(j > 0)
        def _():
            win_ref[lt, pl.ds(off, width - 1), :] = win_ref[lt, pl.ds(tt + off, width - 1), :]

        win_ref[lt, pl.ds(halo, tt), :] = v_ref[0, :, ls]
    gln = gln_ref[...]
    bln = bln_ref[...]
    span = rb + halo - SUBLANE

    def body(i, carry):
        r0 = pl.multiple_of(i * rb, rb)

        def lane_body(lt, inner):
            col = win_ref[lt, pl.ds(r0, rb + halo), :]
            acc = jnp.broadcast_to(wdw_ref[lt, width:width + 1, :], (rb, LANE))
            for s in range(SUBLANE):
                taps = [k for k in range(width) if (off + k) % SUBLANE == s]
                if not taps:
                    continue
                if s > 0:
                    xs_ref[0:span, :] = col[s:s + span, :]
                for k in taps:
                    a0 = (off + k) // SUBLANE * SUBLANE
                    xk = col[a0:a0 + rb, :] if s == 0 else xs_ref[a0:a0 + rb, :]
                    acc = acc + wdw_ref[lt, k:k + 1, :] * xk
            pre_ref[lt, pl.ds(r0, rb), :] = acc
            return inner

        lax.fori_loop(0, nl, lane_body, 0)
        return carry

    lax.fori_loop(0, tt // rb, body, 0)

    def norm_body(i, carry):
        r0 = pl.multiple_of(i * rb, rb)
        acc = jnp.concatenate([pre_ref[lt, pl.ds(r0, rb), :] for lt in range(nl)], axis=-1)
        mu = jnp.mean(acc, axis=-1, keepdims=True)
        dlt = acc - mu
        var = jnp.mean(dlt * dlt, axis=-1, keepdims=True)
        yn = dlt * lax.rsqrt(var + EPS) * gln + bln
        y_ref[0, pl.ds(r0, rb), :] = (yn * _sigmoid(yn)).astype(y_ref.dtype)
        return carry

    lax.fori_loop(0, tt // rb, norm_body, 0, unroll=min(4, tt // rb))

    @pl.when(j == pl.num_programs(1) - 1)
    def _():
        for lt in range(nl):
            ns_ref[0, :, lt * LANE:(lt + 1) * LANE] = win_ref[lt, pl.ds(tv + off, width - 1), :]


def _conv(v, state, w_dw, b_dw, g_ln, b_ln, t_valid):
    b, t, c = v.shape
    width = w_dw.shape[0]
    nl = c // LANE
    halo = -(-(width - 1) // SUBLANE) * SUBLANE
    tt = _tile(t, 512)
    rb = _tile(tt, 64, 2 * SUBLANE)
    nt = t // tt
    tv = t_valid - (nt - 1) * tt
    wp = -(-(width + 1) // SUBLANE) * SUBLANE
    wb = jnp.concatenate([w_dw, b_dw.reshape(1, c), jnp.zeros((wp - width - 1, c), F32)], axis=0)
    wb = wb.reshape(wp, nl, LANE).transpose(1, 0, 2)
    vec = lambda a: a.reshape(1, c)
    cst = lambda shape: pl.BlockSpec(shape, lambda i, j: (0,) * len(shape))
    return pl.pallas_call(
        functools.partial(_conv_kernel, width=width, tt=tt, tv=tv, rb=rb, halo=halo),
        out_shape=(jax.ShapeDtypeStruct((b, t, c), BF16), jax.ShapeDtypeStruct((b, width - 1, c), F32)),
        grid=(b, nt),
        in_specs=[
            pl.BlockSpec((1, tt, c), lambda i, j: (i, j, 0)),
            pl.BlockSpec((1, width - 1, c), lambda i, j: (i, 0, 0)),
            cst((nl, wp, LANE)), cst((1, c)), cst((1, c)),
        ],
        out_specs=(
            pl.BlockSpec((1, tt, c), lambda i, j: (i, j, 0)),
            pl.BlockSpec((1, width - 1, c), lambda i, j: (i, 0, 0)),
        ),
        scratch_shapes=[pltpu.VMEM((nl, tt + halo, LANE), F32), pltpu.VMEM((rb + halo, LANE), F32),
                        pltpu.VMEM((nl, tt, LANE), F32)],
        compiler_params=_cp("parallel", "arbitrary"),
        name="conv",
    )(v, state, wb, vec(g_ln), vec(b_ln))


def _attn_kernel(q_ref, k_ref, v_ref, o_ref, *, tq, vd):
    qi = pl.program_id(2)
    q = q_ref[0]
    row = lax.broadcasted_iota(I32, (tq, tq), 0)
    col = lax.broadcasted_iota(I32, (tq, tq), 1)

    def block(j, carry, masked):
        ks = pl.multiple_of(j * tq, tq)
        kb = k_ref[0, pl.ds(ks, tq), :]
        vb = v_ref[0, pl.ds(ks, tq), :]
        out = []
        for hh in range(2):
            m, l, acc = carry[hh]
            s = lax.dot_general(q[:, hh * LANE:(hh + 1) * LANE], kb[:, hh * LANE:(hh + 1) * LANE],
                                (((1,), (1,)), ((), ())), preferred_element_type=F32)
            if masked:
                s = jnp.where(col <= row, s, NEG)
            m_new = jnp.maximum(m, jnp.max(s, axis=-1, keepdims=True))
            alpha = jnp.exp(m - m_new)
            p = jnp.exp(s - m_new)
            l = alpha * l + jnp.sum(p, axis=-1, keepdims=True)
            acc = alpha * acc + _dot(p.astype(BF16), vb)
            out.append((m_new, l, acc))
        return tuple(out)

    one = (jnp.full((tq, 1), NEG, F32), jnp.zeros((tq, 1), F32), jnp.zeros((tq, LANE), F32))
    carry = lax.fori_loop(0, qi, lambda j, cr: block(j, cr, False), (one, one))
    carry = block(qi, carry, True)
    o0 = carry[0][2] / carry[0][1]
    o1 = carry[1][2] / carry[1][1]
    lane = lax.broadcasted_iota(I32, (tq, LANE), 1)
    o_ref[0] = jnp.where(lane < vd, o0, o1).astype(o_ref.dtype)


def _attention(q, k, v, b, s, h, vd):
    assert 2 * vd == LANE and h % 2 == 0
    tq = _tile(s, 512)
    q3 = q.reshape(b, s, h * LANE)
    k3 = k.reshape(b, s, h * LANE)
    v3 = v.reshape(b, s, h * vd)
    out = pl.pallas_call(
        functools.partial(_attn_kernel, tq=tq, vd=vd),
        out_shape=jax.ShapeDtypeStruct((b, s, h * vd), BF16),
        grid=(b, h // 2, s // tq),
        in_specs=[
            pl.BlockSpec((1, tq, 2 * LANE), lambda i, p, j: (i, j, p)),
            pl.BlockSpec((1, s, 2 * LANE), lambda i, p, j: (i, 0, p)),
            pl.BlockSpec((1, s, LANE), lambda i, p, j: (i, 0, p)),
        ],
        out_specs=pl.BlockSpec((1, tq, LANE), lambda i, p, j: (i, j, p)),
        compiler_params=_cp("parallel", "parallel", "arbitrary"),
        name="attention",
    )(q3, k3, v3)
    return out.reshape(b * s, h * vd)


def _absorb_kernel(q_ref, gkh_ref, wukt_ref, sel_ref, a_ref, *, h, kl):
    aw = kl + LANE
    for hd in range(h):
        qg = (q_ref[:, hd * LANE:(hd + 1) * LANE].astype(F32) * gkh_ref[...]).astype(BF16)
        a_ref[:, hd * aw:hd * aw + kl] = _dot(qg, wukt_ref[hd])
        a_ref[:, hd * aw + kl:(hd + 1) * aw] = _dot(qg, sel_ref[...])


def _absorb(q, gkh_pad, wukt, sel, h, kl):
    n = q.shape[0]
    full = lambda a: pl.BlockSpec(a.shape, lambda i: (0,) * a.ndim)
    return pl.pallas_call(
        functools.partial(_absorb_kernel, h=h, kl=kl),
        out_shape=jax.ShapeDtypeStruct((n, h * (kl + LANE)), F32),
        grid=(1,),
        in_specs=[full(q), full(gkh_pad), full(wukt), full(sel)],
        out_specs=pl.BlockSpec((n, h * (kl + LANE)), lambda i: (0, 0)),
        compiler_params=_cp("arbitrary"),
        name="absorb",
    )(q, gkh_pad, wukt, sel)


def _paged_kernel(pt_ref, a_ref, cn_ref, pn_ref, wukt_ref, sel_ref, wuv_ref, poolc_ref, poolp_ref, o_ref,
                  cbuf, pbuf, sems, lhs_sc, m_sc, l_sc, acc_sc, *, pg, kl, rope, qk, h, nope, vd, t):
    j = pl.program_id(1)
    nj = pl.num_programs(1)
    step = pl.program_id(0) * nj + j
    last = pl.num_programs(0) * nj - 1
    slot = step % 2

    def page_copies(stp, sl):
        out = []
        for u in range(pg):
            page = pt_ref[stp * pg + u]
            out.append(pltpu.make_async_copy(poolc_ref.at[page], cbuf.at[sl, u], sems.at[0, sl]))
            out.append(pltpu.make_async_copy(poolp_ref.at[page], pbuf.at[sl, u], sems.at[1, sl]))
        return out

    @pl.when(step == 0)
    def _():
        for cp in page_copies(0, 0):
            cp.start()

    for cp in page_copies(jnp.minimum(step + 1, last), 1 - slot):
        cp.start()
    pltpu.make_async_copy(poolc_ref.at[pl.ds(0, pg)], cbuf.at[slot], sems.at[0, slot]).wait()
    pltpu.make_async_copy(poolp_ref.at[pl.ds(0, pg)], pbuf.at[slot], sems.at[1, slot]).wait()
    th = t * h
    hn = h * nope
    a = a_ref[0]
    a_p = a[:, kl:kl + rope].astype(BF16)
    inv_qk = 1.0 / qk

    @pl.when(j == 0)
    def _():
        lhs_sc[0:hn, :] = wukt_ref[...]
        lhs_sc[hn:hn + th, :] = a[:, :kl].astype(BF16)
        m_sc[...] = jnp.full(m_sc.shape, NEG, F32)
        l_sc[...] = jnp.zeros(l_sc.shape, F32)
        acc_sc[...] = jnp.zeros(acc_sc.shape, F32)

    def fold(x):
        out = x[0:SUBLANE]
        for r in range(1, x.shape[0] // SUBLANE):
            out = out + x[r * SUBLANE:(r + 1) * SUBLANE]
        return out

    def update(cf, ptf, valid):
        cb = cf.astype(BF16)
        res = lax.dot_general(lhs_sc[...], cb, (((1,), (1,)), ((), ())), preferred_element_type=F32)
        parts = []
        for hd in range(h):
            blk = res[hd * nope:(hd + 1) * nope, :]
            parts.append(fold(blk * blk))
        parts.append(fold(ptf * ptf))
        parts.append(jnp.zeros_like(parts[0]))
        stack = jnp.concatenate(parts, axis=0)
        hi = stack.astype(BF16)
        lo = (stack - hi.astype(F32)).astype(BF16)
        ksq = _dot(sel_ref[...], hi) + _dot(sel_ref[...], lo)
        rinv = lax.rsqrt(ksq * inv_qk + EPS)
        num = res[hn:hn + th, :] + _dot(a_p, ptf.astype(BF16))
        s = jnp.concatenate([num[tt * h:(tt + 1) * h, :] * rinv for tt in range(t)], axis=0)
        if valid is not None:
            s = jnp.where(valid, s, NEG)
        m_old = m_sc[:, 0:1]
        m_new = jnp.maximum(m_old, jnp.max(s, axis=-1, keepdims=True))
        alpha = jnp.exp(m_old - m_new)
        p = jnp.exp(s - m_new)
        l_sc[...] = alpha * l_sc[...] + jnp.sum(p, axis=-1, keepdims=True)
        m_sc[...] = jnp.broadcast_to(m_new, m_sc.shape)
        acc_sc[...] = alpha * acc_sc[...] + _dot(p.astype(BF16), cb)

    ps = cbuf.shape[2]
    cf = cbuf[slot].reshape(pg * ps, kl)
    ptf = jnp.concatenate([pbuf[slot, u] for u in range(pg)], axis=1)
    update(cf, ptf, None)

    @pl.when(step == last)
    def _():
        pltpu.make_async_copy(poolc_ref.at[pl.ds(0, pg)], cbuf.at[1 - slot], sems.at[0, 1 - slot]).wait()
        pltpu.make_async_copy(poolp_ref.at[pl.ds(0, pg)], pbuf.at[1 - slot], sems.at[1, 1 - slot]).wait()

    @pl.when(j == nj - 1)
    def _():
        tn = cn_ref.shape[1]
        kcol = lax.broadcasted_iota(I32, (th, tn), 1)
        qrow = lax.broadcasted_iota(I32, (th, tn), 0) // h
        update(cn_ref[0], pn_ref[0], jnp.logical_and(kcol <= qrow, kcol < t))
        on = (acc_sc[...] / l_sc[:, 0:1]).astype(BF16)
        oa = _dot(on, wuv_ref[...])
        r_h = lax.broadcasted_iota(I32, (th, h * vd), 0) % h
        c_h = lax.broadcasted_iota(I32, (th, h * vd), 1) // vd
        oa = jnp.where(r_h == c_h, oa, 0.0)
        rows = [jnp.sum(oa[tt * h:(tt + 1) * h, :], axis=0, keepdims=True) for tt in range(t)]
        o_ref[0] = jnp.concatenate(rows, axis=0)


def _paged_attention(a, c_new, pe_new, pool_c, pool_pet, page_table, wp, dims):
    bs, th, aw = a.shape
    t = c_new.shape[1]
    kl, rope, qk, h, vd, nope = dims["kl"], dims["rope"], dims["qk"], dims["h"], dims["vd"], dims["nope"]
    assert h == SUBLANE and nope % SUBLANE == 0 and rope % SUBLANE == 0 and t <= LANE
    c_new = jnp.pad(c_new, [(0, 0), (0, LANE - t), (0, 0)])
    pet_new = jnp.pad(jnp.swapaxes(pe_new, 1, 2), [(0, 0), (0, 0), (0, LANE - t)])
    n_pages = page_table.shape[1]
    ps = pool_c.shape[1]
    pg = _tile(n_pages, 16, 1)
    pt = page_table.reshape(-1).astype(I32)

    cst = lambda arr: pl.BlockSpec(arr.shape, lambda i, j, ptr: (0,) * arr.ndim)
    in_specs = [
        pl.BlockSpec((1, th, aw), lambda i, j, ptr: (i, 0, 0)),
        pl.BlockSpec((1, LANE, kl), lambda i, j, ptr: (i, 0, 0)),
        pl.BlockSpec((1, rope, LANE), lambda i, j, ptr: (i, 0, 0)),
        cst(wp["wukt_flat"]), cst(wp["sel_sq"]), cst(wp["wuv"]),
        pl.BlockSpec(memory_space=pl.ANY), pl.BlockSpec(memory_space=pl.ANY),
    ]
    grid_spec = pltpu.PrefetchScalarGridSpec(
        num_scalar_prefetch=1,
        grid=(bs, n_pages // pg),
        in_specs=in_specs,
        out_specs=pl.BlockSpec((1, t, h * vd), lambda i, j, ptr: (i, 0, 0)),
        scratch_shapes=[pltpu.VMEM((2, pg, ps, kl), F32), pltpu.VMEM((2, pg, rope, ps), F32),
                        pltpu.SemaphoreType.DMA((2, 2)),
                        pltpu.VMEM((h * nope + th, kl), BF16), pltpu.VMEM((th, LANE), F32),
                        pltpu.VMEM((th, LANE), F32), pltpu.VMEM((th, kl), F32)],
    )
    return pl.pallas_call(
        functools.partial(_paged_kernel, pg=pg, kl=kl, rope=rope, qk=qk, h=h, nope=nope, vd=vd, t=t),
        out_shape=jax.ShapeDtypeStruct((bs, t, h * vd), F32),
        grid_spec=grid_spec,
        compiler_params=_cp("arbitrary", "arbitrary"),
        name="paged_attention",
    )(pt, a, c_new, pet_new, wp["wukt_flat"], wp["sel_sq"], wp["wuv"], pool_c, pool_pet)


def _mixer_kernel(x_ref, yc_ref, at_ref, sh1_ref, sc1_ref, g1_ref, sh2_ref, sc2_ref, gmix_ref, wg_ref, wco_ref,
                  wao_ref, wout_ref, gffn_ref, wrh_ref, wrl_ref, br_ref,
                  xm_ref, h2_ref, idx_ref, prob_ref, *, d, e):
    x = x_ref[...]
    hmod = _rms(x, gmix_ref[...]) * (1.0 + sc1_ref[0]) + sh1_ref[0]
    gates = _dot(hmod.astype(BF16), wg_ref[...])
    conv_out = _dot(yc_ref[...].astype(BF16), wco_ref[...])
    attn_out = _dot(at_ref[...].astype(BF16), wao_ref[...])
    merged = _sigmoid(gates[:, :d]) * conv_out + _sigmoid(gates[:, d:]) * attn_out
    xm = x + g1_ref[0] * _dot(merged.astype(BF16), wout_ref[...])
    xm_ref[...] = xm
    h2 = _rms(xm, gffn_ref[...]) * (1.0 + sc2_ref[0]) + sh2_ref[0]
    h2_ref[...] = _pack_rows(h2)
    hi = h2.astype(BF16)
    lo = (h2 - hi.astype(F32)).astype(BF16)
    lg = _dot(hi, wrh_ref[...]) + _dot(lo, wrh_ref[...]) + _dot(hi, wrl_ref[...]) + br_ref[...]
    tm = x.shape[0]
    lane = lax.broadcasted_iota(I32, (tm, LANE), 1)
    lane_f = lane.astype(F32)
    lg = jnp.where(lane < e, lg, -jnp.inf)
    vals, idxs = [], []
    for _ in range(TOP_K):
        m = jnp.max(lg, axis=-1, keepdims=True)
        ik = jnp.min(jnp.where(lg == m, lane_f, float(LANE)), axis=-1, keepdims=True)
        vals.append(m)
        idxs.append(ik)
        lg = jnp.where(lane_f == ik, -jnp.inf, lg)
    ex = [jnp.exp(vk - vals[0]) for vk in vals]
    den = ex[0]
    for ek in ex[1:]:
        den = den + ek
    idx_o = jnp.zeros((tm, LANE), F32)
    prob_o = jnp.zeros((tm, LANE), F32)
    for k in range(TOP_K):
        idx_o = jnp.where(lane == k, idxs[k], idx_o)
        prob_o = jnp.where(lane == k, ex[k] / den, prob_o)
    idx_ref[...] = idx_o.astype(I32)
    prob_ref[...] = prob_o


def _mixer(x, yc, at, mods, mod_map, wp, dims, tm):
    n, d = x.shape
    e = dims["e"]
    mblk = (1,) + mods[0].shape[1:]
    full = lambda a: pl.BlockSpec(a.shape, lambda i: (0,) * a.ndim)
    row = lambda w: pl.BlockSpec((tm, w), lambda i: (i, 0))
    mspec = pl.BlockSpec(mblk, mod_map)
    weights = [wp["gmix"], wp["wg"], wp["wco"], wp["wao"], wp["wout"], wp["gffn"], wp["wrh"], wp["wrl"], wp["br"]]
    return pl.pallas_call(
        functools.partial(_mixer_kernel, d=d, e=e),
        out_shape=(
            jax.ShapeDtypeStruct((n, d), F32),
            jax.ShapeDtypeStruct((n, d // 2), U32),
            jax.ShapeDtypeStruct((n, LANE), I32),
            jax.ShapeDtypeStruct((n, LANE), F32),
        ),
        grid=(n // tm,),
        in_specs=[row(d), row(yc.shape[1]), row(at.shape[1]), mspec, mspec, mspec, mspec, mspec]
        + [full(w) for w in weights],
        out_specs=(row(d), row(d // 2), row(LANE), row(LANE)),
        compiler_params=_cp("parallel"),
        name="mixer",
    )(x, yc, at, mods[0], mods[1], mods[2], mods[3], mods[4], *weights)


def _lane_cumsum(x):
    lane = lax.broadcasted_iota(I32, x.shape, 1)
    s = 1
    while s < LANE:
        x = x + jnp.where(lane >= s, pltpu.roll(x, s, axis=1), 0.0)
        s *= 2
    return x


def _route_kernel(idx_ref, dest_ref, meta_ref, run_ref, tot_ref, *, e, blk, nbp, tm):
    ph = pl.program_id(0)
    i = pl.program_id(1)
    lane = lax.broadcasted_iota(I32, (tm, LANE), 1)
    idx = idx_ref[...]
    oh = [lane == idx[:, k:k + 1] for k in range(TOP_K)]
    sel = jnp.zeros((tm, LANE), F32)
    for k in range(TOP_K):
        sel = sel + oh[k].astype(F32)
    tile_cnt = jnp.sum(sel, axis=0, keepdims=True)

    @pl.when(jnp.logical_and(ph == 0, i == 0))
    def _():
        run_ref[...] = jnp.zeros(run_ref.shape, F32)

    @pl.when(ph == 0)
    def _():
        run_ref[...] = run_ref[...] + tile_cnt

    @pl.when(jnp.logical_and(ph == 1, i == 0))
    def _():
        cnt = run_ref[...]
        gend = _lane_cumsum(cnt)
        gstart = gend - cnt
        tot_ref[...] = gstart
        run_ref[...] = jnp.zeros(run_ref.shape, F32)
        inv = 1.0 / blk
        first_b = jnp.floor(gstart * inv)
        last_b = jnp.floor((gend - 1.0) * inv)
        n_it = jnp.where(cnt > 0.0, last_b - first_b + 1.0, 0.0)
        it_end = _lane_cumsum(n_it)
        it_off = it_end - n_it
        lane_w = lax.broadcasted_iota(I32, (nbp, LANE), 1)
        w = lax.broadcasted_iota(I32, (nbp, LANE), 0).astype(F32)[:, 0:1]
        ex = jnp.sum(jnp.where(jnp.logical_and(lane_w < e, it_end[0:1, :] <= w), 1.0, 0.0), axis=-1, keepdims=True)
        ex = jnp.minimum(ex, float(e - 1))
        ohw = lane_w.astype(F32) == ex
        pick = lambda v: jnp.sum(jnp.where(ohw, v[0:1, :], 0.0), axis=-1, keepdims=True)
        bk = pick(first_b) + (w - pick(it_off))
        lo = jnp.maximum(pick(gstart), bk * blk) - bk * blk
        hi = jnp.minimum(pick(gend), (bk + 1.0) * blk) - bk * blk
        n_items = jnp.max(it_end[0:1, :], axis=-1, keepdims=True)
        meta = jnp.zeros((nbp, LANE), F32)
        for col, val in enumerate((ex, bk, lo, hi, n_items)):
            meta = jnp.where(lane_w == col, val, meta)
        meta_ref[...] = meta.astype(I32)

    @pl.when(ph == 1)
    def _():
        r = lax.broadcasted_iota(I32, (tm, tm), 0)
        c = lax.broadcasted_iota(I32, (tm, tm), 1)
        tri = jnp.where(c < r, 1.0, 0.0).astype(BF16)
        prefix = _dot(tri, sel.astype(BF16))
        base = prefix + run_ref[0:1, :] + tot_ref[0:1, :]
        dest = jnp.zeros((tm, LANE), F32)
        for k in range(TOP_K):
            dk = jnp.sum(jnp.where(oh[k], base, 0.0), axis=-1, keepdims=True)
            dest = jnp.where(lane == k, dk, dest)
        dest_ref[...] = dest.astype(I32)
        run_ref[...] = run_ref[...] + tile_cnt


def _route(idx, e, blk, nbp):
    n = idx.shape[0]
    tm = _tile(n, 256)
    assert tm <= 256
    return pl.pallas_call(
        functools.partial(_route_kernel, e=e, blk=blk, nbp=nbp, tm=tm),
        out_shape=(jax.ShapeDtypeStruct((n, LANE), I32), jax.ShapeDtypeStruct((nbp, LANE), I32)),
        grid=(2, n // tm),
        in_specs=[pl.BlockSpec((tm, LANE), lambda ph, i: (i, 0))],
        out_specs=(pl.BlockSpec((tm, LANE), lambda ph, i: (i * ph, 0)),
                   pl.BlockSpec((nbp, LANE), lambda ph, i: (0, 0))),
        scratch_shapes=[pltpu.VMEM((SUBLANE, LANE), F32), pltpu.VMEM((SUBLANE, LANE), F32)],
        compiler_params=_cp("arbitrary", "arbitrary"),
        name="route",
    )(idx)


def _dispatch_kernel(dest_ref, h_ref, buf_ref, sem, *, tm):
    def body(g, carry):
        t0 = pl.multiple_of(g * SUBLANE, SUBLANE)
        for u in range(SUBLANE):
            for k in range(TOP_K):
                dst = dest_ref[t0 * TOP_K + (u * TOP_K + k)]
                pltpu.make_async_copy(h_ref.at[pl.ds(t0 + u, 1), :], buf_ref.at[pl.ds(dst, 1), :],
                                      sem).start(priority=k % 2)
        return carry

    lax.fori_loop(0, tm // SUBLANE, body, 0)
    for k in range(TOP_K):
        pltpu.make_async_copy(h_ref, buf_ref.at[pl.ds(0, tm), :], sem).wait()


def _dispatch(h2, dest_flat, rows):
    n, d = h2.shape
    tm = 256 if n % 256 == 0 else n
    return pl.pallas_call(
        functools.partial(_dispatch_kernel, tm=tm),
        out_shape=jax.ShapeDtypeStruct((rows, d), h2.dtype),
        grid=(n // tm,),
        in_specs=[pl.BlockSpec((tm * TOP_K,), lambda i: (i,), memory_space=pltpu.SMEM),
                  pl.BlockSpec((tm, d), lambda i: (i, 0))],
        out_specs=pl.BlockSpec(memory_space=pl.ANY),
        scratch_shapes=[pltpu.SemaphoreType.DMA(())],
        compiler_params=_cp("arbitrary"),
        name="dispatch",
    )(dest_flat, h2)


def _expert_kernel(ex_ref, bk_ref, lo_ref, hi_ref, ni_ref, x_ref, wup_ref, bup_ref, wdn_ref, bdn_ref, y_ref, *, blk,
                   f):
    w = pl.program_id(0)
    del ex_ref

    @pl.when(w < ni_ref[0])
    def _():
        rowi = lax.broadcasted_iota(I32, (blk, 1), 0)
        inside = jnp.logical_and(rowi >= lo_ref[w], rowi < hi_ref[w])
        xa, xb = _unpack_rows(jnp.where(inside, x_ref[...], jnp.uint32(0)))
        x = jnp.concatenate([xa, xb], axis=1).astype(BF16)
        gu = _dot(x, wup_ref[0]) + bup_ref[0]
        gate = jnp.minimum(gu[:, :f], SWIGLU_LIMIT)
        lin = jnp.clip(gu[:, f:], -SWIGLU_LIMIT, SWIGLU_LIMIT)
        act = gate * _sigmoid(SWIGLU_ALPHA * gate) * (lin + 1.0)
        y = _pack_rows(_dot(act.astype(BF16), wdn_ref[0]) + bdn_ref[0])
        first = jnp.logical_or(w == 0, bk_ref[w] != bk_ref[jnp.maximum(w - 1, 0)])

        @pl.when(first)
        def _():
            y_ref[...] = jnp.where(inside, y, jnp.uint32(0))

        @pl.when(jnp.logical_not(first))
        def _():
            y_ref[...] = jnp.where(inside, y, y_ref[...])


def _experts(buf, meta, w_up, b_up, w_dn, b_dn, blk, n_items_max):
    rows, dh = buf.shape
    d = 2 * dh
    e, _, f2 = w_up.shape
    f = f2 // 2
    ex, bk, lo, hi = (meta[:, c] for c in range(4))
    ni = meta[:1, 4]

    def item(w, ni_r):
        return jnp.minimum(w, ni_r[0] - 1)

    def blk_map(w, ex_r, bk_r, lo_r, hi_r, ni_r):
        return (bk_r[item(w, ni_r)], 0)

    def exp_map(w, ex_r, bk_r, lo_r, hi_r, ni_r):
        return (ex_r[item(w, ni_r)], 0, 0)

    grid_spec = pltpu.PrefetchScalarGridSpec(
        num_scalar_prefetch=5,
        grid=(n_items_max,),
        in_specs=[
            pl.BlockSpec((blk, dh), blk_map),
            pl.BlockSpec((1, d, f2), exp_map),
            pl.BlockSpec((1, 1, f2), exp_map),
            pl.BlockSpec((1, f, d), exp_map),
            pl.BlockSpec((1, 1, d), exp_map),
        ],
        out_specs=pl.BlockSpec((blk, dh), blk_map),
    )
    return pl.pallas_call(
        functools.partial(_expert_kernel, blk=blk, f=f),
        out_shape=jax.ShapeDtypeStruct((rows, dh), U32),
        grid_spec=grid_spec,
        compiler_params=_cp("arbitrary"),
        name="experts",
    )(ex, bk, lo, hi, ni, buf, w_up, b_up.reshape(e, 1, f2), w_dn, b_dn.reshape(e, 1, d))


def _combine_kernel(dcur_ref, dnxt_ref, x_ref, g2_ref, p_ref, y_ref, o_ref, yg_ref, sem, *, tm):
    i = pl.program_id(0)
    n = pl.num_programs(0)
    slot = i % 2

    def issue(dref, s):
        def body(g, carry):
            t0 = pl.multiple_of(g * SUBLANE, SUBLANE)
            for u in range(SUBLANE):
                for k in range(TOP_K):
                    src = dref[t0 * TOP_K + (u * TOP_K + k)]
                    pltpu.make_async_copy(y_ref.at[pl.ds(src, 1), :], yg_ref.at[s, k, pl.ds(t0 + u, 1), :],
                                          sem.at[s]).start(priority=k % 2)
            return carry

        lax.fori_loop(0, tm // SUBLANE, body, 0)

    @pl.when(i == 0)
    def _():
        issue(dcur_ref, 0)

    @pl.when(i + 1 < n)
    def _():
        issue(dnxt_ref, 1 - slot)

    for k in range(TOP_K):
        pltpu.make_async_copy(y_ref.at[pl.ds(0, tm), :], yg_ref.at[slot, k], sem.at[slot]).wait()
    prob = p_ref[...]
    dh = yg_ref.shape[3]
    acc_a = jnp.zeros((tm, dh), F32)
    acc_b = jnp.zeros((tm, dh), F32)
    for k in range(TOP_K):
        ya, yb = _unpack_rows(yg_ref[slot, k])
        acc_a = acc_a + prob[:, k:k + 1] * ya
        acc_b = acc_b + prob[:, k:k + 1] * yb
    g2 = g2_ref[0]
    o_ref[:, :dh] = x_ref[:, :dh] + g2[:, :dh] * acc_a
    o_ref[:, dh:] = x_ref[:, dh:] + g2[:, dh:] * acc_b


def _combine(xm, g2, mod_map, prob, ybuf, dest_flat):
    n, d = xm.shape
    tm = 256 if n % 256 == 0 else n
    nt = n // tm
    mblk = (1,) + g2.shape[1:]
    return pl.pallas_call(
        functools.partial(_combine_kernel, tm=tm),
        out_shape=jax.ShapeDtypeStruct((n, d), F32),
        grid=(nt,),
        in_specs=[
            pl.BlockSpec((tm * TOP_K,), lambda i: (i,), memory_space=pltpu.SMEM),
            pl.BlockSpec((tm * TOP_K,), lambda i: (jnp.minimum(i + 1, nt - 1),), memory_space=pltpu.SMEM),
            pl.BlockSpec((tm, d), lambda i: (i, 0)),
            pl.BlockSpec(mblk, mod_map),
            pl.BlockSpec((tm, LANE), lambda i: (i, 0)),
            pl.BlockSpec(memory_space=pl.ANY),
        ],
        out_specs=pl.BlockSpec((tm, d), lambda i: (i, 0)),
        scratch_shapes=[pltpu.VMEM((2, TOP_K, tm, d // 2), U32), pltpu.SemaphoreType.DMA((2,))],
        compiler_params=_cp("arbitrary"),
        name="combine",
    )(dest_flat, dest_flat, xm, g2, prob, ybuf)


def _rope_tables(pos, nope, rope, qk):
    p = pos.shape[0]
    inv_freq = ROPE_THETA ** (-jnp.arange(0, rope, 2, dtype=F32) / rope)
    ang = pos.astype(F32)[:, None] * inv_freq
    c2 = jnp.concatenate([jnp.cos(ang)] * 2, axis=-1)
    s2 = jnp.concatenate([jnp.sin(ang)] * 2, axis=-1)
    cq = jnp.concatenate([jnp.ones((p, nope), F32), c2, jnp.ones((p, LANE - qk), F32)], axis=-1)
    sq = jnp.concatenate([jnp.zeros((p, nope), F32), s2, jnp.zeros((p, LANE - qk), F32)], axis=-1)
    ck = jnp.concatenate([c2, jnp.zeros((p, LANE - rope), F32)], axis=-1)
    sk = jnp.concatenate([s2, jnp.zeros((p, LANE - rope), F32)], axis=-1)
    return cq, sq, ck, sk


def _swap_halves(w):
    half = w.shape[-1] // 2
    return jnp.concatenate([-w[..., half:], w[..., :half]], axis=-1)


def _pad_last(w, n):
    return jnp.pad(w, [(0, 0)] * (w.ndim - 1) + [(0, n - w.shape[-1])])


def _prepare(p, dims, t_s):
    (g_mix, g_ffn, w_in, g_ql, w_uq, g_kvl, w_uk, w_uv, g_qh, g_kh, w_ao, w_out, w_co, w_r, b_r) = p
    d, c, ql, kl, h = dims["d"], dims["c"], dims["ql"], dims["kl"], dims["h"]
    nope, rope, qk, vd, e = dims["nope"], dims["rope"], dims["qk"], dims["vd"], dims["e"]
    o_q = 2 * c
    o_kv = o_q + ql
    o_kr = o_kv + kl
    o_ga = o_kr + rope
    w_kr = w_in[:, o_kr:o_ga]
    w1 = jnp.concatenate([w_in[:, :o_kr], _pad_last(w_kr, LANE), _pad_last(_swap_halves(w_kr), LANE)], axis=1)
    wq_plain = _pad_last(w_uq, LANE)
    wq_swap = jnp.concatenate([jnp.zeros((ql, h, nope), F32), _swap_halves(w_uq[..., nope:]),
                               jnp.zeros((ql, h, LANE - qk), F32)], axis=-1)
    wq = jnp.concatenate([wq_plain.reshape(ql, h * LANE), wq_swap.reshape(ql, h * LANE)], axis=1)
    sm_scale = qk ** -0.5
    place = jnp.zeros((LANE, LANE), F32).at[jnp.arange(rope), nope + jnp.arange(rope)].set(1.0)
    epe = jnp.tile(place, (1, h))
    wukt = _pad_last(jnp.transpose(w_uk, (1, 2, 0)), kl)
    wukt = jnp.pad(wukt, [(0, 0), (0, LANE - nope), (0, 0)])
    sel = jnp.zeros((LANE, LANE), F32).at[nope + jnp.arange(rope), jnp.arange(rope)].set(1.0)
    col = jnp.arange((h + 2) * SUBLANE) // SUBLANE
    sel_sq = jnp.logical_or(col[None, :] == jnp.arange(h)[:, None], col[None, :] == h).astype(BF16)
    hi = w_r.astype(BF16)
    lo = (w_r - hi.astype(F32)).astype(BF16)
    row = lambda g: g.reshape(1, -1).astype(F32)
    return dict(
        gmix=row(g_mix), gffn=row(g_ffn), w1=w1.astype(BF16), gql=row(g_ql), wq=wq.astype(BF16),
        gqh=row(_pad_last(g_qh, LANE)) * sm_scale, gkvl=row(g_kvl),
        wuk=_pad_last(w_uk, LANE).reshape(kl, h * LANE).astype(BF16), epe=epe.astype(BF16),
        gkh=row(_pad_last(g_kh, LANE)), wuv=w_uv.reshape(kl, h * vd).astype(BF16),
        wukt=wukt.astype(BF16), sel=sel.astype(BF16), wukt_flat=w_uk.reshape(kl, h * nope).T.astype(BF16),
        sel_sq=sel_sq,
        wg=w_in[:, o_ga:].astype(BF16), wco=w_co.astype(BF16), wao=w_ao.astype(BF16), wout=w_out.astype(BF16),
        wrh=_pad_last(hi, LANE), wrl=_pad_last(lo, LANE), br=row(_pad_last(b_r, LANE)),
    )


def _moe(xm, h2, idx, prob, g2, mod_map, w_up, b_up, w_dn, b_dn, e):
    n = xm.shape[0]
    n_asg = n * TOP_K
    blk = 512 if n_asg >= 2048 * e else 128
    assert n_asg % blk == 0
    n_items_max = n_asg // blk + e - 1
    nbp = -(-n_items_max // SUBLANE) * SUBLANE
    dest, meta = _route(idx, e, blk, nbp)
    dest_flat = dest[:, :TOP_K].reshape(-1)
    buf = _dispatch(h2, dest_flat, n_asg)
    ybuf = _experts(buf, meta, w_up, b_up, w_dn, b_dn, blk, n_items_max)
    return _combine(xm, g2, mod_map, prob, ybuf, dest_flat)


def _layer(x, mod, pos_tabs, conv_state, attend, wp, ex, dims, group):
    n, d = x.shape
    tm = _tile(n, 512)
    if mod[0].shape[1] == 1:
        per = group // tm
        mod_map = lambda i: (i // per, 0, 0)
        tab_map = lambda i: (i % per, 0)
    else:
        mod_map = lambda i: (i, 0, 0)
        tab_map = lambda i: (i, 0)
    shift1, scale1, gate1, shift2, scale2, gate2 = mod
    v, q, ckv, kpe, k, vv = _inproj(x, (shift1, scale1), mod_map, pos_tabs, tab_map, wp, dims, tm)
    yc, new_state = conv_state(v)
    at = attend(q, ckv, kpe, k, vv)
    xm, h2, idx, prob = _mixer(x, yc, at, (shift1, scale1, gate1, shift2, scale2), mod_map, wp, dims, tm)
    if mod[0].shape[1] == 1:
        tmc = 256 if n % 256 == 0 else n
        perc = group // tmc
        comb_map = lambda i: (i // perc, 0, 0)
        g2 = gate2
    else:
        tmc = 256 if n % 256 == 0 else n
        comb_map = lambda i: (i, 0, 0)
        g2 = gate2.reshape(n // tmc, tmc, d)
    y = _moe(xm, h2, idx, prob, g2, comb_map, *ex, dims["e"])
    return y, ckv, kpe, new_state


def kernel(x_prompt, x_sample, cache_kv_latent, cache_k_rope, state_conv, page_table, c_prompt, c_sample, w_ada, b_ada, g_norm_mix, g_norm_ffn, w_in, w_dw, b_dw, g_conv_ln, b_conv_ln, w_conv_out, g_q_lat, w_uq, g_kv_lat, w_uk, w_uv, g_q_head, g_k_head, w_attn_out, w_out, w_router, b_router, w_up, b_up, w_down, b_down):
    b, s, d = x_prompt.shape
    bs, ts, _ = x_sample.shape
    depth = w_ada.shape[0]
    width, c = w_dw.shape[1], w_dw.shape[2]
    ql, h, qk = w_uq.shape[1], w_uq.shape[2], w_uq.shape[3]
    kl, nope, vd = w_uk.shape[1], w_uk.shape[3], w_uv.shape[3]
    rope = qk - nope
    e = w_router.shape[2]
    n_pages, ps = page_table.shape[1], cache_kv_latent.shape[2]
    past_len = n_pages * ps
    dims = dict(d=d, c=c, ql=ql, kl=kl, h=h, nope=nope, rope=rope, qk=qk, vd=vd, e=e)
    assert qk <= LANE and rope % 2 == 0 and e <= LANE and s >= width - 1
    assert all(v % LANE == 0 for v in (d, c, ql, kl)) and ts * h <= LANE

    n_p, n_s = b * s, bs * ts
    tabs_p = _rope_tables(jnp.arange(s, dtype=I32), nope, rope, qk)
    tabs_s = _rope_tables(jnp.tile(past_len + jnp.arange(ts, dtype=I32), bs), nope, rope, qk)
    tm_s = _tile(n_s, 512)
    ts_pad = -(-ts // (2 * SUBLANE)) * (2 * SUBLANE)

    yp = x_prompt.reshape(n_p, d)
    ys = x_sample.reshape(n_s, d)
    outs = [[] for _ in range(6)]
    for l in range(depth):
        wp = _prepare((g_norm_mix[l], g_norm_ffn[l], w_in[l], g_q_lat[l], w_uq[l], g_kv_lat[l], w_uk[l], w_uv[l],
                       g_q_head[l], g_k_head[l], w_attn_out[l], w_out[l], w_conv_out[l], w_router[l], b_router[l]),
                      dims, ts)
        ex = (w_up[l].astype(BF16), b_up[l], w_down[l].astype(BF16), b_down[l])
        mod = _adaln(jnp.concatenate([c_prompt, c_sample], axis=0), w_ada[l], b_ada[l]).reshape(b + bs, N_MOD, d)
        mod_p = [mod[:b, j].reshape(b, 1, d) for j in range(N_MOD)]
        mod_s = [jnp.repeat(mod[b:, j], ts, axis=0).reshape(n_s // tm_s, tm_s, d) for j in range(N_MOD)]

        def conv_p(v):
            yc, ns = _conv(v.reshape(b, s, c), jnp.zeros((b, width - 1, c), F32), w_dw[l], b_dw[l], g_conv_ln[l],
                           b_conv_ln[l], s)
            return yc.reshape(n_p, c), ns

        def conv_s(v):
            vp = jnp.pad(v.reshape(bs, ts, c), [(0, 0), (0, ts_pad - ts), (0, 0)])
            yc, ns = _conv(vp, state_conv[l], w_dw[l], b_dw[l], g_conv_ln[l], b_conv_ln[l], ts)
            return yc[:, :ts].reshape(n_s, c), ns

        def attend_p(q, ckv, kpe, k, vv):
            return _attention(q, k, vv, b, s, h, vd)

        def attend_s(q, ckv, kpe, k, vv):
            a = _absorb(q, wp["gkh"], wp["wukt"], wp["sel"], h, kl).reshape(bs, ts * h, kl + LANE)
            o = _paged_attention(a, ckv.reshape(bs, ts, kl), kpe.reshape(bs, ts, rope), cache_kv_latent[l],
                                 jnp.swapaxes(cache_k_rope[l], 1, 2), page_table, wp, dims)
            return o.reshape(n_s, h * vd)

        yp, c_p, r_p, s_p = _layer(yp, mod_p, tabs_p, conv_p, attend_p, wp, ex, dims, s)
        ys, c_s, r_s, s_s = _layer(ys, mod_s, tabs_s, conv_s, attend_s, wp, ex, dims, tm_s)
        for lst, val in zip(outs, (c_p.reshape(b, s, kl), r_p.reshape(b, s, rope), s_p, c_s.reshape(bs, ts, kl),
                                   r_s.reshape(bs, ts, rope), s_s)):
            lst.append(val)
    return (yp.reshape(b, s, d), ys.reshape(bs, ts, d)) + tuple(jnp.stack(o) for o in outs)
```

```python
import functools

import jax
import jax.numpy as jnp
from jax import lax
from jax.experimental import pallas as pl
from jax.experimental.pallas import tpu as pltpu

F32 = jnp.float32
BF16 = jnp.bfloat16
I32 = jnp.int32
U32 = jnp.uint32

EPS = 1e-6
TOP_K = 4
ROPE_THETA = 10000.0
SWIGLU_LIMIT = 7.0
SWIGLU_ALPHA = 1.702
N_MOD = 6

LANE = 128
SUBLANE = 8
VMEM_LIMIT_BYTES = 56 * 1024 * 1024
NEG = -1e30


def _cp(*sem):
    return pltpu.CompilerParams(dimension_semantics=sem, vmem_limit_bytes=VMEM_LIMIT_BYTES)


def _tile(n, pref, mult=SUBLANE):
    t = min(pref, n)
    t -= t % mult
    while t >= mult:
        if n % t == 0:
            return t
        t -= mult
    return n


def _sigmoid(x):
    return 1.0 / (1.0 + jnp.exp(-x))


def _rms(x, g):
    return x * lax.rsqrt(jnp.mean(x * x, axis=-1, keepdims=True) + EPS) * g


def _dot(a, b):
    return jnp.dot(a, b, preferred_element_type=F32)


def _store_row_tiles(ref, x):
    r = x.shape[0]
    nc = x.shape[1] // LANE
    for c in range(nc):
        ref[pl.ds(c, r, stride=nc), :] = x[:, c * LANE:(c + 1) * LANE]


def _load_row_tiles(ref, nc):
    r = ref.shape[0] // nc
    return jnp.concatenate([ref[pl.ds(c, r, stride=nc), :] for c in range(nc)], axis=1)


def _adaln_kernel(c_ref, w_ref, b_ref, o_ref):
    c = c_ref[...]
    s = (c * _sigmoid(c)).astype(BF16)
    o_ref[...] = _dot(s, w_ref[...].astype(BF16)) + b_ref[...]


def _adaln(c, w_ada, b_ada):
    r, d = c.shape
    n = w_ada.shape[1]
    tn = _tile(n, 1536, LANE)
    return pl.pallas_call(
        _adaln_kernel,
        out_shape=jax.ShapeDtypeStruct((r, n), F32),
        grid=(n // tn,),
        in_specs=[
            pl.BlockSpec((r, d), lambda j: (0, 0)),
            pl.BlockSpec((d, tn), lambda j: (0, j)),
            pl.BlockSpec((1, tn), lambda j: (0, j)),
        ],
        out_specs=pl.BlockSpec((r, tn), lambda j: (0, j)),
        compiler_params=_cp("arbitrary"),
        name="adaln",
    )(c, w_ada, b_ada.reshape(1, n))


def _inproj_kernel(x_ref, sh_ref, sc_ref, cq_ref, sq_ref, ck_ref, sk_ref, gmix_ref, w1_ref, gql_ref, wq_ref,
                   gqh_ref, gkvl_ref, wuk_ref, epe_ref, gkh_ref, wuv_ref,
                   v_ref, q_ref, ckv_ref, kpe_ref, k_ref, vv_ref, *, c, ql, kl, h, qk, rope):
    x = x_ref[...]
    hmod = _rms(x, gmix_ref[...]) * (1.0 + sc_ref[0]) + sh_ref[0]
    p = _dot(hmod.astype(BF16), w1_ref[...])
    v_ref[...] = p[:, :c] * _sigmoid(p[:, c:2 * c])
    o_q = 2 * c
    o_kv = o_q + ql
    o_kr = o_kv + kl
    qn = _rms(p[:, o_q:o_kv], gql_ref[...])
    qq = _dot(qn.astype(BF16), wq_ref[...])
    cq = cq_ref[...]
    sq = sq_ref[...]
    inv_qk = 1.0 / qk
    for hd in range(h):
        qr = qq[:, hd * LANE:(hd + 1) * LANE] * cq + qq[:, (h + hd) * LANE:(h + hd + 1) * LANE] * sq
        ms = jnp.sum(qr * qr, axis=-1, keepdims=True) * inv_qk
        q_ref[:, hd * LANE:(hd + 1) * LANE] = (qr * lax.rsqrt(ms + EPS) * gqh_ref[...]).astype(q_ref.dtype)
    ckv = _rms(p[:, o_kv:o_kr], gkvl_ref[...])
    ckv_ref[...] = ckv
    kpe = p[:, o_kr:o_kr + LANE] * ck_ref[...] + p[:, o_kr + LANE:o_kr + 2 * LANE] * sk_ref[...]
    kpe_ref[...] = kpe[:, :rope]
    cb = ckv.astype(BF16)
    kf = _dot(cb, wuk_ref[...]) + _dot(kpe.astype(BF16), epe_ref[...])
    for hd in range(h):
        kh = kf[:, hd * LANE:(hd + 1) * LANE]
        ms = jnp.sum(kh * kh, axis=-1, keepdims=True) * inv_qk
        k_ref[:, hd * LANE:(hd + 1) * LANE] = (kh * lax.rsqrt(ms + EPS) * gkh_ref[...]).astype(k_ref.dtype)
    vv_ref[...] = _dot(cb, wuv_ref[...]).astype(vv_ref.dtype)


def _inproj(x, mods, mod_map, tabs, tab_map, wp, dims, tm):
    n, d = x.shape
    c, ql, kl, h, qk, rope, vd = dims["c"], dims["ql"], dims["kl"], dims["h"], dims["qk"], dims["rope"], dims["vd"]
    mblk = (1,) + mods[0].shape[1:]
    full = lambda a: pl.BlockSpec(a.shape, lambda i: (0,) * a.ndim)
    row = lambda w: pl.BlockSpec((tm, w), lambda i: (i, 0))
    tab = pl.BlockSpec((tm, LANE), tab_map)
    weights = [wp["gmix"], wp["w1"], wp["gql"], wp["wq"], wp["gqh"], wp["gkvl"], wp["wuk"], wp["epe"], wp["gkh"],
               wp["wuv"]]
    return pl.pallas_call(
        functools.partial(_inproj_kernel, c=c, ql=ql, kl=kl, h=h, qk=qk, rope=rope),
        out_shape=(
            jax.ShapeDtypeStruct((n, c), F32),
            jax.ShapeDtypeStruct((n, h * LANE), BF16),
            jax.ShapeDtypeStruct((n, kl), F32),
            jax.ShapeDtypeStruct((n, rope), F32),
            jax.ShapeDtypeStruct((n, h * LANE), BF16),
            jax.ShapeDtypeStruct((n, h * vd), BF16),
        ),
        grid=(n // tm,),
        in_specs=[row(d), pl.BlockSpec(mblk, mod_map), pl.BlockSpec(mblk, mod_map), tab, tab, tab, tab]
        + [full(w) for w in weights],
        out_specs=(row(c), row(h * LANE), row(kl), row(rope), row(h * LANE), row(h * vd)),
        compiler_params=_cp("parallel"),
        name="inproj",
    )(x, mods[0], mods[1], *tabs, *weights)


def _conv_kernel(v_ref, st_ref, wdw_ref, gln_ref, bln_ref, y_ref, ns_ref, win_ref, xs_ref, pre_ref, *, width, tt,
                 tv, rb, halo):
    j = pl.program_id(1)
    off = halo - (width - 1)
    nl = win_ref.shape[0]
    for lt in range(nl):
        ls = slice(lt * LANE, (lt + 1) * LANE)

        @pl.when(j == 0)
        def _():
            win_ref[lt, pl.ds(off, width - 1), :] = st_ref[0, :, ls]

        @pl.when(j > 0)
        def _():
            win_ref[lt, pl.ds(off, width - 1), :] = win_ref[lt, pl.ds(tt + off, width - 1), :]

        win_ref[lt, pl.ds(halo, tt), :] = v_ref[0, :, ls]
    gln = gln_ref[...]
    bln = bln_ref[...]
    span = rb + halo - SUBLANE

    def body(i, carry):
        r0 = pl.multiple_of(i * rb, rb)

        def lane_body(lt, inner):
            col = win_ref[lt, pl.ds(r0, rb + halo), :]
            acc = jnp.broadcast_to(wdw_ref[lt, width:width + 1, :], (rb, LANE))
            for s in range(SUBLANE):
                taps = [k for k in range(width) if (off + k) % SUBLANE == s]
                if not taps:
                    continue
                if s > 0:
                    xs_ref[0:span, :] = col[s:s + span, :]
                for k in taps:
                    a0 = (off + k) // SUBLANE * SUBLANE
                    xk = col[a0:a0 + rb, :] if s == 0 else xs_ref[a0:a0 + rb, :]
                    acc = acc + wdw_ref[lt, k:k + 1, :] * xk
            pre_ref[lt, pl.ds(r0, rb), :] = acc
            return inner

        lax.fori_loop(0, nl, lane_body, 0)
        return carry

    lax.fori_loop(0, tt // rb, body, 0)

    def norm_body(i, carry):
        r0 = pl.multiple_of(i * rb, rb)
        acc = jnp.concatenate([pre_ref[lt, pl.ds(r0, rb), :] for lt in range(nl)], axis=-1)
        mu = jnp.mean(acc, axis=-1, keepdims=True)
        dlt = acc - mu
        var = jnp.mean(dlt * dlt, axis=-1, keepdims=True)
        yn = dlt * lax.rsqrt(var + EPS) * gln + bln
        y_ref[0, pl.ds(r0, rb), :] = (yn * _sigmoid(yn)).astype(y_ref.dtype)
        return carry

    lax.fori_loop(0, tt // rb, norm_body, 0, unroll=min(4, tt // rb))

    @pl.when(j == pl.num_programs(1) - 1)
    def _():
        for lt in range(nl):
            ns_ref[0, :, lt * LANE:(lt + 1) * LANE] = win_ref[lt, pl.ds(tv + off, width - 1), :]


def _conv(v, state, w_dw, b_dw, g_ln, b_ln, t_valid):
    b, t, c = v.shape
    width = w_dw.shape[0]
    nl = c // LANE
    halo = -(-(width - 1) // SUBLANE) * SUBLANE
    tt = _tile(t, 512)
    rb = _tile(tt, 64, 2 * SUBLANE)
    nt = t // tt
    tv = t_valid - (nt - 1) * tt
    wp = -(-(width + 1) // SUBLANE) * SUBLANE
    wb = jnp.concatenate([w_dw, b_dw.reshape(1, c), jnp.zeros((wp - width - 1, c), F32)], axis=0)
    wb = wb.reshape(wp, nl, LANE).transpose(1, 0, 2)
    vec = lambda a: a.reshape(1, c)
    cst = lambda shape: pl.BlockSpec(shape, lambda i, j: (0,) * len(shape))
    return pl.pallas_call(
        functools.partial(_conv_kernel, width=width, tt=tt, tv=tv, rb=rb, halo=halo),
        out_shape=(jax.ShapeDtypeStruct((b, t, c), BF16), jax.ShapeDtypeStruct((b, width - 1, c), F32)),
        grid=(b, nt),
        in_specs=[
            pl.BlockSpec((1, tt, c), lambda i, j: (i, j, 0)),
            pl.BlockSpec((1, width - 1, c), lambda i, j: (i, 0, 0)),
            cst((nl, wp, LANE)), cst((1, c)), cst((1, c)),
        ],
        out_specs=(
            pl.BlockSpec((1, tt, c), lambda i, j: (i, j, 0)),
            pl.BlockSpec((1, width - 1, c), lambda i, j: (i, 0, 0)),
        ),
        scratch_shapes=[pltpu.VMEM((nl, tt + halo, LANE), F32), pltpu.VMEM((rb + halo, LANE), F32),
                        pltpu.VMEM((nl, tt, LANE), F32)],
        compiler_params=_cp("parallel", "arbitrary"),
        name="conv",
    )(v, state, wb, vec(g_ln), vec(b_ln))


def _attn_kernel(q_ref, k_ref, v_ref, o_ref, *, tq, vd):
    qi = pl.program_id(2)
    q = q_ref[0]
    row = lax.broadcasted_iota(I32, (tq, tq), 0)
    col = lax.broadcasted_iota(I32, (tq, tq), 1)

    def block(j, carry, masked):
        ks = pl.multiple_of(j * tq, tq)
        kb = k_ref[0, pl.ds(ks, tq), :]
        vb = v_ref[0, pl.ds(ks, tq), :]
        out = []
        for hh in range(2):
            m, l, acc = carry[hh]
            s = lax.dot_general(q[:, hh * LANE:(hh + 1) * LANE], kb[:, hh * LANE:(hh + 1) * LANE],
                                (((1,), (1,)), ((), ())), preferred_element_type=F32)
            if masked:
                s = jnp.where(col <= row, s, NEG)
            m_new = jnp.maximum(m, jnp.max(s, axis=-1, keepdims=True))
            alpha = jnp.exp(m - m_new)
            p = jnp.exp(s - m_new)
            l = alpha * l + jnp.sum(p, axis=-1, keepdims=True)
            acc = alpha * acc + _dot(p.astype(BF16), vb)
            out.append((m_new, l, acc))
        return tuple(out)

    one = (jnp.full((tq, 1), NEG, F32), jnp.zeros((tq, 1), F32), jnp.zeros((tq, LANE), F32))
    carry = lax.fori_loop(0, qi, lambda j, cr: block(j, cr, False), (one, one))
    carry = block(qi, carry, True)
    o0 = carry[0][2] / carry[0][1]
    o1 = carry[1][2] / carry[1][1]
    lane = lax.broadcasted_iota(I32, (tq, LANE), 1)
    o_ref[0] = jnp.where(lane < vd, o0, o1).astype(o_ref.dtype)


def _attention(q, k, v, b, s, h, vd):
    assert 2 * vd == LANE and h % 2 == 0
    tq = _tile(s, 512)
    q3 = q.reshape(b, s, h * LANE)
    k3 = k.reshape(b, s, h * LANE)
    v3 = v.reshape(b, s, h * vd)
    out = pl.pallas_call(
        functools.partial(_attn_kernel, tq=tq, vd=vd),
        out_shape=jax.ShapeDtypeStruct((b, s, h * vd), BF16),
        grid=(b, h // 2, s // tq),
        in_specs=[
            pl.BlockSpec((1, tq, 2 * LANE), lambda i, p, j: (i, j, p)),
            pl.BlockSpec((1, s, 2 * LANE), lambda i, p, j: (i, 0, p)),
            pl.BlockSpec((1, s, LANE), lambda i, p, j: (i, 0, p)),
        ],
        out_specs=pl.BlockSpec((1, tq, LANE), lambda i, p, j: (i, j, p)),
        compiler_params=_cp("parallel", "parallel", "arbitrary"),
        name="attention",
    )(q3, k3, v3)
    return out.reshape(b * s, h * vd)


def _absorb_kernel(q_ref, gkh_ref, wukt_ref, sel_ref, a_ref, *, h, kl):
    aw = kl + LANE
    for hd in range(h):
        qg = (q_ref[:, hd * LANE:(hd + 1) * LANE].astype(F32) * gkh_ref[...]).astype(BF16)
        a_ref[:, hd * aw:hd * aw + kl] = _dot(qg, wukt_ref[hd])
        a_ref[:, hd * aw + kl:(hd + 1) * aw] = _dot(qg, sel_ref[...])


def _absorb(q, gkh_pad, wukt, sel, h, kl):
    n = q.shape[0]
    full = lambda a: pl.BlockSpec(a.shape, lambda i: (0,) * a.ndim)
    return pl.pallas_call(
        functools.partial(_absorb_kernel, h=h, kl=kl),
        out_shape=jax.ShapeDtypeStruct((n, h * (kl + LANE)), F32),
        grid=(1,),
        in_specs=[full(q), full(gkh_pad), full(wukt), full(sel)],
        out_specs=pl.BlockSpec((n, h * (kl + LANE)), lambda i: (0, 0)),
        compiler_params=_cp("arbitrary"),
        name="absorb",
    )(q, gkh_pad, wukt, sel)


def _paged_kernel(pt_ref, a_ref, cn_ref, pn_ref, wukt_ref, sel_ref, wuv_ref, poolc_ref, poolp_ref, o_ref,
                  cbuf, pbuf, sems, lhs_sc, m_sc, l_sc, acc_sc, *, pg, kl, rope, qk, h, nope, vd, t):
    j = pl.program_id(1)
    nj = pl.num_programs(1)
    step = pl.program_id(0) * nj + j
    last = pl.num_programs(0) * nj - 1
    slot = step % 2

    def page_copies(stp, sl):
        out = []
        for u in range(pg):
            page = pt_ref[stp * pg + u]
            out.append(pltpu.make_async_copy(poolc_ref.at[page], cbuf.at[sl, u], sems.at[0, sl]))
            out.append(pltpu.make_async_copy(poolp_ref.at[page], pbuf.at[sl, u], sems.at[1, sl]))
        return out

    def start_all(copies):
        for n_cp, cp in enumerate(copies):
            cp.start(priority=(n_cp // 2) % 2)

    @pl.when(step == 0)
    def _():
        start_all(page_copies(0, 0))

    start_all(page_copies(jnp.minimum(step + 1, last), 1 - slot))
    pltpu.make_async_copy(poolc_ref.at[pl.ds(0, pg)], cbuf.at[slot], sems.at[0, slot]).wait()
    pltpu.make_async_copy(poolp_ref.at[pl.ds(0, pg)], pbuf.at[slot], sems.at[1, slot]).wait()
    th = t * h
    hn = h * nope
    a = a_ref[0]
    a_p = a[:, kl:kl + rope].astype(BF16)
    inv_qk = 1.0 / qk

    @pl.when(j == 0)
    def _():
        lhs_sc[0:hn, :] = wukt_ref[...]
        lhs_sc[hn:hn + th, :] = a[:, :kl].astype(BF16)
        m_sc[...] = jnp.full(m_sc.shape, NEG, F32)
        l_sc[...] = jnp.zeros(l_sc.shape, F32)
        acc_sc[...] = jnp.zeros(acc_sc.shape, F32)

    def fold(x):
        out = x[0:SUBLANE]
        for r in range(1, x.shape[0] // SUBLANE):
            out = out + x[r * SUBLANE:(r + 1) * SUBLANE]
        return out

    def scores(cb, ptf):
        res = lax.dot_general(lhs_sc[...], cb, (((1,), (1,)), ((), ())), preferred_element_type=F32)
        parts = []
        for hd in range(h):
            blk = res[hd * nope:(hd + 1) * nope, :]
            parts.append(fold(blk * blk))
        parts.append(fold(ptf * ptf))
        parts.append(jnp.zeros_like(parts[0]))
        stack = jnp.concatenate(parts, axis=0)
        hi = stack.astype(BF16)
        lo = (stack - hi.astype(F32)).astype(BF16)
        ksq = _dot(sel_ref[...], hi) + _dot(sel_ref[...], lo)
        rinv = lax.rsqrt(ksq * inv_qk + EPS)
        num = res[hn:hn + th, :] + _dot(a_p, ptf.astype(BF16))
        return jnp.concatenate([num[tt * h:(tt + 1) * h, :] * rinv for tt in range(t)], axis=0)

    def update(cf, ptf, valid, n_sub):
        cb = cf.astype(BF16)
        sub = cb.shape[0] // n_sub
        s = jnp.concatenate([scores(cb[u * sub:(u + 1) * sub, :], ptf[:, u * sub:(u + 1) * sub])
                             for u in range(n_sub)], axis=1)
        if valid is not None:
            s = jnp.where(valid, s, NEG)
        m_old = m_sc[:, 0:1]
        m_new = jnp.maximum(m_old, jnp.max(s, axis=-1, keepdims=True))
        alpha = jnp.exp(m_old - m_new)
        p = jnp.exp(s - m_new)
        l_sc[...] = alpha * l_sc[...] + jnp.sum(p, axis=-1, keepdims=True)
        m_sc[...] = jnp.broadcast_to(m_new, m_sc.shape)
        acc_sc[...] = alpha * acc_sc[...] + _dot(p.astype(BF16), cb)

    ps = cbuf.shape[2]
    cf = cbuf[slot].reshape(pg * ps, kl)
    ptf = jnp.concatenate([pbuf[slot, u] for u in range(pg)], axis=1)
    update(cf, ptf, None, 1)

    @pl.when(step == last)
    def _():
        pltpu.make_async_copy(poolc_ref.at[pl.ds(0, pg)], cbuf.at[1 - slot], sems.at[0, 1 - slot]).wait()
        pltpu.make_async_copy(poolp_ref.at[pl.ds(0, pg)], pbuf.at[1 - slot], sems.at[1, 1 - slot]).wait()

    @pl.when(j == nj - 1)
    def _():
        tn = cn_ref.shape[1]
        kcol = lax.broadcasted_iota(I32, (th, tn), 1)
        qrow = lax.broadcasted_iota(I32, (th, tn), 0) // h
        update(cn_ref[0], pn_ref[0], jnp.logical_and(kcol <= qrow, kcol < t), 1)
        on = (acc_sc[...] / l_sc[:, 0:1]).astype(BF16)
        oa = _dot(on, wuv_ref[...])
        r_h = lax.broadcasted_iota(I32, (th, h * vd), 0) % h
        c_h = lax.broadcasted_iota(I32, (th, h * vd), 1) // vd
        oa = jnp.where(r_h == c_h, oa, 0.0)
        rows = [jnp.sum(oa[tt * h:(tt + 1) * h, :], axis=0, keepdims=True) for tt in range(t)]
        o_ref[0] = jnp.concatenate(rows, axis=0)


def _paged_attention(a, c_new, pe_new, pool_c, pool_pet, page_table, wp, dims):
    bs, th, aw = a.shape
    t = c_new.shape[1]
    kl, rope, qk, h, vd, nope = dims["kl"], dims["rope"], dims["qk"], dims["h"], dims["vd"], dims["nope"]
    assert h == SUBLANE and nope % SUBLANE == 0 and rope % SUBLANE == 0 and t <= LANE
    c_new = jnp.pad(c_new, [(0, 0), (0, LANE - t), (0, 0)])
    pet_new = jnp.pad(jnp.swapaxes(pe_new, 1, 2), [(0, 0), (0, 0), (0, LANE - t)])
    n_pages = page_table.shape[1]
    ps = pool_c.shape[1]
    pg = _tile(n_pages, 32, 1)
    pt = page_table.reshape(-1).astype(I32)

    cst = lambda arr: pl.BlockSpec(arr.shape, lambda i, j, ptr: (0,) * arr.ndim)
    in_specs = [
        pl.BlockSpec((1, th, aw), lambda i, j, ptr: (i, 0, 0)),
        pl.BlockSpec((1, LANE, kl), lambda i, j, ptr: (i, 0, 0)),
        pl.BlockSpec((1, rope, LANE), lambda i, j, ptr: (i, 0, 0)),
        cst(wp["wukt_flat"]), cst(wp["sel_sq"]), cst(wp["wuv"]),
        pl.BlockSpec(memory_space=pl.ANY), pl.BlockSpec(memory_space=pl.ANY),
    ]
    grid_spec = pltpu.PrefetchScalarGridSpec(
        num_scalar_prefetch=1,
        grid=(bs, n_pages // pg),
        in_specs=in_specs,
        out_specs=pl.BlockSpec((1, t, h * vd), lambda i, j, ptr: (i, 0, 0)),
        scratch_shapes=[pltpu.VMEM((2, pg, ps, kl), F32), pltpu.VMEM((2, pg, rope, ps), F32),
                        pltpu.SemaphoreType.DMA((2, 2)),
                        pltpu.VMEM((h * nope + th, kl), BF16), pltpu.VMEM((th, LANE), F32),
                        pltpu.VMEM((th, LANE), F32), pltpu.VMEM((th, kl), F32)],
    )
    return pl.pallas_call(
        functools.partial(_paged_kernel, pg=pg, kl=kl, rope=rope, qk=qk, h=h, nope=nope, vd=vd, t=t),
        out_shape=jax.ShapeDtypeStruct((bs, t, h * vd), F32),
        grid_spec=grid_spec,
        compiler_params=_cp("arbitrary", "arbitrary"),
        name="paged_attention",
    )(pt, a, c_new, pet_new, wp["wukt_flat"], wp["sel_sq"], wp["wuv"], pool_c, pool_pet)


def _mixer_kernel(x_ref, yc_ref, at_ref, sh1_ref, sc1_ref, g1_ref, sh2_ref, sc2_ref, gmix_ref, wg_ref, wco_ref,
                  wao_ref, wout_ref, gffn_ref, wrh_ref, wrl_ref, br_ref, tri_ref,
                  xm_ref, h2_ref, idx_ref, prob_ref, pos_ref, cnt_ref, run_ref, *, d, e):
    x = x_ref[...]
    hmod = _rms(x, gmix_ref[...]) * (1.0 + sc1_ref[0]) + sh1_ref[0]
    gates = _dot(hmod.astype(BF16), wg_ref[...])
    conv_out = _dot(yc_ref[...].astype(BF16), wco_ref[...])
    attn_out = _dot(at_ref[...].astype(BF16), wao_ref[...])
    merged = _sigmoid(gates[:, :d]) * conv_out + _sigmoid(gates[:, d:]) * attn_out
    xm = x + g1_ref[0] * _dot(merged.astype(BF16), wout_ref[...])
    xm_ref[...] = xm
    h2 = _rms(xm, gffn_ref[...]) * (1.0 + sc2_ref[0]) + sh2_ref[0]
    _store_row_tiles(h2_ref, h2)
    hi = h2.astype(BF16)
    lo = (h2 - hi.astype(F32)).astype(BF16)
    lg = _dot(hi, wrh_ref[...]) + _dot(lo, wrh_ref[...]) + _dot(hi, wrl_ref[...]) + br_ref[...]
    tm = x.shape[0]
    lane = lax.broadcasted_iota(I32, (tm, LANE), 1)
    lane_f = lane.astype(F32)
    lg = jnp.where(lane < e, lg, -jnp.inf)
    vals, idxs = [], []
    for _ in range(TOP_K):
        m = jnp.max(lg, axis=-1, keepdims=True)
        ik = jnp.min(jnp.where(lg == m, lane_f, float(LANE)), axis=-1, keepdims=True)
        vals.append(m)
        idxs.append(ik)
        lg = jnp.where(lane_f == ik, -jnp.inf, lg)
    ex = [jnp.exp(vk - vals[0]) for vk in vals]
    den = ex[0]
    for ek in ex[1:]:
        den = den + ek
    @pl.when(pl.program_id(0) == 0)
    def _():
        run_ref[...] = jnp.zeros(run_ref.shape, F32)

    ohs = [lane_f == ik for ik in idxs]
    sel = ohs[0].astype(F32)
    for oh in ohs[1:]:
        sel = sel + oh.astype(F32)
    before = _dot(tri_ref[...], sel.astype(BF16)) + run_ref[0:1, :]
    idx_o = jnp.zeros((tm, LANE), F32)
    prob_o = jnp.zeros((tm, LANE), F32)
    pos_o = jnp.zeros((tm, LANE), F32)
    for k in range(TOP_K):
        idx_o = jnp.where(lane == k, idxs[k], idx_o)
        prob_o = jnp.where(lane == k, ex[k] / den, prob_o)
        pos_o = jnp.where(lane == k, jnp.sum(jnp.where(ohs[k], before, 0.0), axis=-1, keepdims=True), pos_o)
    idx_ref[...] = idx_o.astype(I32)
    prob_ref[...] = prob_o
    pos_ref[...] = pos_o.astype(I32)
    run_ref[...] = run_ref[...] + jnp.sum(sel, axis=0, keepdims=True)
    cnt_ref[...] = run_ref[...]


def _mixer(x, yc, at, mods, mod_map, wp, dims, tm):
    n, d = x.shape
    e = dims["e"]
    tri = (jnp.arange(tm)[None, :] < jnp.arange(tm)[:, None]).astype(BF16)
    mblk = (1,) + mods[0].shape[1:]
    full = lambda a: pl.BlockSpec(a.shape, lambda i: (0,) * a.ndim)
    row = lambda w: pl.BlockSpec((tm, w), lambda i: (i, 0))
    mspec = pl.BlockSpec(mblk, mod_map)
    weights = [wp["gmix"], wp["wg"], wp["wco"], wp["wao"], wp["wout"], wp["gffn"], wp["wrh"], wp["wrl"], wp["br"],
               tri]
    return pl.pallas_call(
        functools.partial(_mixer_kernel, d=d, e=e),
        out_shape=(
            jax.ShapeDtypeStruct((n, d), F32),
            jax.ShapeDtypeStruct((n * (d // LANE), LANE), F32),
            jax.ShapeDtypeStruct((n, LANE), I32),
            jax.ShapeDtypeStruct((n, LANE), F32),
            jax.ShapeDtypeStruct((n, LANE), I32),
            jax.ShapeDtypeStruct((SUBLANE, LANE), F32),
        ),
        grid=(n // tm,),
        in_specs=[row(d), row(yc.shape[1]), row(at.shape[1]), mspec, mspec, mspec, mspec, mspec]
        + [full(w) for w in weights],
        out_specs=(row(d), pl.BlockSpec((tm * (d // LANE), LANE), lambda i: (i, 0)), row(LANE), row(LANE),
                   row(LANE), pl.BlockSpec((SUBLANE, LANE), lambda i: (0, 0))),
        scratch_shapes=[pltpu.VMEM((SUBLANE, LANE), F32)],
        compiler_params=_cp("arbitrary"),
        name="mixer",
    )(x, yc, at, mods[0], mods[1], mods[2], mods[3], mods[4], *weights)


def _lane_cumsum(x):
    lane = lax.broadcasted_iota(I32, x.shape, 1)
    s = 1
    while s < LANE:
        x = x + jnp.where(lane >= s, pltpu.roll(x, s, axis=1), 0.0)
        s *= 2
    return x


def _route_kernel(idx_ref, pos_ref, cnt_ref, dest_ref, meta_ref, *, e, blk, nbp, tm):
    i = pl.program_id(0)
    cnt = cnt_ref[...]
    gend = _lane_cumsum(cnt)
    gstart = gend - cnt

    @pl.when(i == 0)
    def _():
        inv = 1.0 / blk
        first_b = jnp.floor(gstart * inv)
        last_b = jnp.floor((gend - 1.0) * inv)
        n_it = jnp.where(cnt > 0.0, last_b - first_b + 1.0, 0.0)
        it_end = _lane_cumsum(n_it)
        it_off = it_end - n_it
        lane_w = lax.broadcasted_iota(I32, (nbp, LANE), 1)
        w = lax.broadcasted_iota(I32, (nbp, LANE), 0).astype(F32)[:, 0:1]
        ex = jnp.sum(jnp.where(jnp.logical_and(lane_w < e, it_end[0:1, :] <= w), 1.0, 0.0), axis=-1, keepdims=True)
        ex = jnp.minimum(ex, float(e - 1))
        ohw = lane_w.astype(F32) == ex
        pick = lambda v: jnp.sum(jnp.where(ohw, v[0:1, :], 0.0), axis=-1, keepdims=True)
        bk = pick(first_b) + (w - pick(it_off))
        lo = jnp.maximum(pick(gstart), bk * blk) - bk * blk
        hi = jnp.minimum(pick(gend), (bk + 1.0) * blk) - bk * blk
        n_items = jnp.max(it_end[0:1, :], axis=-1, keepdims=True)
        meta = jnp.zeros((nbp, LANE), F32)
        for col, val in enumerate((ex, bk, lo, hi, n_items)):
            meta = jnp.where(lane_w == col, val, meta)
        meta_ref[...] = meta.astype(I32)

    lane = lax.broadcasted_iota(I32, (tm, LANE), 1)
    idx = idx_ref[...]
    pos = pos_ref[...].astype(F32)
    dest = jnp.zeros((tm, LANE), F32)
    for k in range(TOP_K):
        gk = jnp.sum(jnp.where(lane == idx[:, k:k + 1], gstart[0:1, :], 0.0), axis=-1, keepdims=True)
        dest = jnp.where(lane == k, gk + pos[:, k:k + 1], dest)
    dest_ref[...] = dest.astype(I32)


def _route(idx, pos, cnt, e, blk, nbp):
    n = idx.shape[0]
    tm = _tile(n, 1024)
    return pl.pallas_call(
        functools.partial(_route_kernel, e=e, blk=blk, nbp=nbp, tm=tm),
        out_shape=(jax.ShapeDtypeStruct((n, LANE), I32), jax.ShapeDtypeStruct((nbp, LANE), I32)),
        grid=(n // tm,),
        in_specs=[pl.BlockSpec((tm, LANE), lambda i: (i, 0)), pl.BlockSpec((tm, LANE), lambda i: (i, 0)),
                  pl.BlockSpec((SUBLANE, LANE), lambda i: (0, 0))],
        out_specs=(pl.BlockSpec((tm, LANE), lambda i: (i, 0)), pl.BlockSpec((nbp, LANE), lambda i: (0, 0))),
        compiler_params=_cp("arbitrary"),
        name="route",
    )(idx, pos, cnt)


def _dispatch_kernel(dest_ref, h_ref, buf_ref, sem, *, tm, nc):
    def body(g, carry):
        t0 = pl.multiple_of(g * SUBLANE, SUBLANE)
        for u in range(SUBLANE):
            for k in range(TOP_K):
                dst = dest_ref[t0 * TOP_K + (u * TOP_K + k)]
                src = pl.ds(pl.multiple_of((t0 + u) * nc, nc), nc)
                pltpu.make_async_copy(h_ref.at[src, :], buf_ref.at[pl.ds(pl.multiple_of(dst * nc, nc), nc), :],
                                      sem).start(priority=k % 2)
        return carry

    lax.fori_loop(0, tm // SUBLANE, body, 0)
    for k in range(TOP_K):
        pltpu.make_async_copy(h_ref, buf_ref.at[pl.ds(0, tm * nc), :], sem).wait()


def _dispatch(h2, dest_flat, rows, nc):
    n = h2.shape[0] // nc
    tm = 256 if n % 256 == 0 else n
    return pl.pallas_call(
        functools.partial(_dispatch_kernel, tm=tm, nc=nc),
        out_shape=jax.ShapeDtypeStruct((rows * nc, LANE), h2.dtype),
        grid=(n // tm,),
        in_specs=[pl.BlockSpec((tm * TOP_K,), lambda i: (i,), memory_space=pltpu.SMEM),
                  pl.BlockSpec((tm * nc, LANE), lambda i: (i, 0))],
        out_specs=pl.BlockSpec(memory_space=pl.ANY),
        scratch_shapes=[pltpu.SemaphoreType.DMA(())],
        compiler_params=_cp("arbitrary"),
        name="dispatch",
    )(dest_flat, h2)


def _expert_kernel(ex_ref, bk_ref, lo_ref, hi_ref, ni_ref, x_ref, wup_ref, bup_ref, wdn_ref, bdn_ref, y_ref,
                   yacc_ref, *, blk, f):
    w = pl.program_id(0)
    del ex_ref

    @pl.when(w == 0)
    def _():
        yacc_ref[...] = jnp.zeros(yacc_ref.shape, F32)

    @pl.when(w < ni_ref[0])
    def _():
        rowi = lax.broadcasted_iota(I32, (blk, 1), 0)
        inside = jnp.logical_and(rowi >= lo_ref[w], rowi < hi_ref[w])
        nc = wup_ref.shape[1] // LANE
        x = jnp.where(inside, _load_row_tiles(x_ref, nc), 0.0).astype(BF16)
        gu = _dot(x, wup_ref[0]) + bup_ref[0]
        gate = jnp.minimum(gu[:, :f], SWIGLU_LIMIT)
        lin = jnp.clip(gu[:, f:], -SWIGLU_LIMIT, SWIGLU_LIMIT)
        act = gate * _sigmoid(SWIGLU_ALPHA * gate) * (lin + 1.0)
        y = _dot(act.astype(BF16), wdn_ref[0]) + bdn_ref[0]
        first = jnp.logical_or(w == 0, bk_ref[w] != bk_ref[jnp.maximum(w - 1, 0)])
        y = jnp.where(inside, y, jnp.where(first, 0.0, yacc_ref[...]))
        yacc_ref[...] = y
        _store_row_tiles(y_ref, y)


def _experts(buf, meta, w_up, b_up, w_dn, b_dn, blk, n_items_max):
    e, d, f2 = w_up.shape
    nc = d // LANE
    f = f2 // 2
    ex, bk, lo, hi = (meta[:, c] for c in range(4))
    ni = meta[:1, 4]

    def item(w, ni_r):
        return jnp.minimum(w, ni_r[0] - 1)

    def blk_map(w, ex_r, bk_r, lo_r, hi_r, ni_r):
        return (bk_r[item(w, ni_r)], 0)

    def exp_map(w, ex_r, bk_r, lo_r, hi_r, ni_r):
        return (ex_r[item(w, ni_r)], 0, 0)

    grid_spec = pltpu.PrefetchScalarGridSpec(
        num_scalar_prefetch=5,
        grid=(n_items_max,),
        in_specs=[
            pl.BlockSpec((blk * nc, LANE), blk_map),
            pl.BlockSpec((1, d, f2), exp_map),
            pl.BlockSpec((1, 1, f2), exp_map),
            pl.BlockSpec((1, f, d), exp_map),
            pl.BlockSpec((1, 1, d), exp_map),
        ],
        out_specs=pl.BlockSpec((blk * nc, LANE), blk_map),
        scratch_shapes=[pltpu.VMEM((blk, d), F32)],
    )
    return pl.pallas_call(
        functools.partial(_expert_kernel, blk=blk, f=f),
        out_shape=jax.ShapeDtypeStruct(buf.shape, F32),
        grid_spec=grid_spec,
        compiler_params=_cp("arbitrary"),
        name="experts",
    )(ex, bk, lo, hi, ni, buf, w_up, b_up.reshape(e, 1, f2), w_dn, b_dn.reshape(e, 1, d))


def _combine_kernel(dcur_ref, dnxt_ref, x_ref, g2_ref, p_ref, y_ref, o_ref, yg_ref, sem, *, tm, nc):
    i = pl.program_id(0)
    n = pl.num_programs(0)
    slot = i % 2

    def issue(dref, s):
        def body(g, carry):
            t0 = pl.multiple_of(g * SUBLANE, SUBLANE)
            for u in range(SUBLANE):
                for k in range(TOP_K):
                    src = dref[t0 * TOP_K + (u * TOP_K + k)]
                    pltpu.make_async_copy(y_ref.at[pl.ds(pl.multiple_of(src * nc, nc), nc), :],
                                          yg_ref.at[s, k, pl.ds(pl.multiple_of((t0 + u) * nc, nc), nc), :],
                                          sem.at[s]).start(priority=k % 2)
            return carry

        lax.fori_loop(0, tm // SUBLANE, body, 0)

    @pl.when(i == 0)
    def _():
        issue(dcur_ref, 0)

    @pl.when(i + 1 < n)
    def _():
        issue(dnxt_ref, 1 - slot)

    for k in range(TOP_K):
        pltpu.make_async_copy(y_ref.at[pl.ds(0, tm * nc), :], yg_ref.at[slot, k], sem.at[slot]).wait()
    prob = p_ref[...]
    acc = prob[:, 0:1] * _load_row_tiles(yg_ref.at[slot, 0], nc)
    for k in range(1, TOP_K):
        acc = acc + prob[:, k:k + 1] * _load_row_tiles(yg_ref.at[slot, k], nc)
    o_ref[...] = x_ref[...] + g2_ref[0] * acc


def _combine(xm, g2, mod_map, prob, ybuf, dest_flat):
    n, d = xm.shape
    nc = d // LANE
    tm = 256 if n % 256 == 0 else n
    nt = n // tm
    mblk = (1,) + g2.shape[1:]
    return pl.pallas_call(
        functools.partial(_combine_kernel, tm=tm, nc=nc),
        out_shape=jax.ShapeDtypeStruct((n, d), F32),
        grid=(nt,),
        in_specs=[
            pl.BlockSpec((tm * TOP_K,), lambda i: (i,), memory_space=pltpu.SMEM),
            pl.BlockSpec((tm * TOP_K,), lambda i: (jnp.minimum(i + 1, nt - 1),), memory_space=pltpu.SMEM),
            pl.BlockSpec((tm, d), lambda i: (i, 0)),
            pl.BlockSpec(mblk, mod_map),
            pl.BlockSpec((tm, LANE), lambda i: (i, 0)),
            pl.BlockSpec(memory_space=pl.ANY),
        ],
        out_specs=pl.BlockSpec((tm, d), lambda i: (i, 0)),
        scratch_shapes=[pltpu.VMEM((2, TOP_K, tm * nc, LANE), F32), pltpu.SemaphoreType.DMA((2,))],
        compiler_params=_cp("arbitrary"),
        name="combine",
    )(dest_flat, dest_flat, xm, g2, prob, ybuf)


def _rope_tables(pos, nope, rope, qk):
    p = pos.shape[0]
    inv_freq = ROPE_THETA ** (-jnp.arange(0, rope, 2, dtype=F32) / rope)
    ang = pos.astype(F32)[:, None] * inv_freq
    c2 = jnp.concatenate([jnp.cos(ang)] * 2, axis=-1)
    s2 = jnp.concatenate([jnp.sin(ang)] * 2, axis=-1)
    cq = jnp.concatenate([jnp.ones((p, nope), F32), c2, jnp.ones((p, LANE - qk), F32)], axis=-1)
    sq = jnp.concatenate([jnp.zeros((p, nope), F32), s2, jnp.zeros((p, LANE - qk), F32)], axis=-1)
    ck = jnp.concatenate([c2, jnp.zeros((p, LANE - rope), F32)], axis=-1)
    sk = jnp.concatenate([s2, jnp.zeros((p, LANE - rope), F32)], axis=-1)
    return cq, sq, ck, sk


def _swap_halves(w):
    half = w.shape[-1] // 2
    return jnp.concatenate([-w[..., half:], w[..., :half]], axis=-1)


def _pad_last(w, n):
    return jnp.pad(w, [(0, 0)] * (w.ndim - 1) + [(0, n - w.shape[-1])])


def _prepare(p, dims, t_s):
    (g_mix, g_ffn, w_in, g_ql, w_uq, g_kvl, w_uk, w_uv, g_qh, g_kh, w_ao, w_out, w_co, w_r, b_r) = p
    d, c, ql, kl, h = dims["d"], dims["c"], dims["ql"], dims["kl"], dims["h"]
    nope, rope, qk, vd, e = dims["nope"], dims["rope"], dims["qk"], dims["vd"], dims["e"]
    o_q = 2 * c
    o_kv = o_q + ql
    o_kr = o_kv + kl
    o_ga = o_kr + rope
    w_kr = w_in[:, o_kr:o_ga]
    w1 = jnp.concatenate([w_in[:, :o_kr], _pad_last(w_kr, LANE), _pad_last(_swap_halves(w_kr), LANE)], axis=1)
    wq_plain = _pad_last(w_uq, LANE)
    wq_swap = jnp.concatenate([jnp.zeros((ql, h, nope), F32), _swap_halves(w_uq[..., nope:]),
                               jnp.zeros((ql, h, LANE - qk), F32)], axis=-1)
    wq = jnp.concatenate([wq_plain.reshape(ql, h * LANE), wq_swap.reshape(ql, h * LANE)], axis=1)
    sm_scale = qk ** -0.5
    place = jnp.zeros((LANE, LANE), F32).at[jnp.arange(rope), nope + jnp.arange(rope)].set(1.0)
    epe = jnp.tile(place, (1, h))
    wukt = _pad_last(jnp.transpose(w_uk, (1, 2, 0)), kl)
    wukt = jnp.pad(wukt, [(0, 0), (0, LANE - nope), (0, 0)])
    sel = jnp.zeros((LANE, LANE), F32).at[nope + jnp.arange(rope), jnp.arange(rope)].set(1.0)
    col = jnp.arange((h + 2) * SUBLANE) // SUBLANE
    sel_sq = jnp.logical_or(col[None, :] == jnp.arange(h)[:, None], col[None, :] == h).astype(BF16)
    hi = w_r.astype(BF16)
    lo = (w_r - hi.astype(F32)).astype(BF16)
    row = lambda g: g.reshape(1, -1).astype(F32)
    return dict(
        gmix=row(g_mix), gffn=row(g_ffn), w1=w1.astype(BF16), gql=row(g_ql), wq=wq.astype(BF16),
        gqh=row(_pad_last(g_qh, LANE)) * sm_scale, gkvl=row(g_kvl),
        wuk=_pad_last(w_uk, LANE).reshape(kl, h * LANE).astype(BF16), epe=epe.astype(BF16),
        gkh=row(_pad_last(g_kh, LANE)), wuv=w_uv.reshape(kl, h * vd).astype(BF16),
        wukt=wukt.astype(BF16), sel=sel.astype(BF16), wukt_flat=w_uk.reshape(kl, h * nope).T.astype(BF16),
        sel_sq=sel_sq,
        wg=w_in[:, o_ga:].astype(BF16), wco=w_co.astype(BF16), wao=w_ao.astype(BF16), wout=w_out.astype(BF16),
        wrh=_pad_last(hi, LANE), wrl=_pad_last(lo, LANE), br=row(_pad_last(b_r, LANE)),
    )


def _moe(xm, h2, idx, prob, pos, cnt, g2, mod_map, w_up, b_up, w_dn, b_dn, e):
    n = xm.shape[0]
    n_asg = n * TOP_K
    blk = 512 if n_asg >= 2048 * e else 128
    assert n_asg % blk == 0
    n_items_max = n_asg // blk + e - 1
    nbp = -(-n_items_max // SUBLANE) * SUBLANE
    dest, meta = _route(idx, pos, cnt, e, blk, nbp)
    dest_flat = dest[:, :TOP_K].reshape(-1)
    buf = _dispatch(h2, dest_flat, n_asg, xm.shape[1] // LANE)
    ybuf = _experts(buf, meta, w_up, b_up, w_dn, b_dn, blk, n_items_max)
    return _combine(xm, g2, mod_map, prob, ybuf, dest_flat)


def _layer(x, mod, pos_tabs, conv_state, attend, wp, ex, dims, group):
    n, d = x.shape
    tm = _tile(n, 512)
    if mod[0].shape[1] == 1:
        per = group // tm
        mod_map = lambda i: (i // per, 0, 0)
        tab_map = lambda i: (i % per, 0)
    else:
        mod_map = lambda i: (i, 0, 0)
        tab_map = lambda i: (i, 0)
    shift1, scale1, gate1, shift2, scale2, gate2 = mod
    v, q, ckv, kpe, k, vv = _inproj(x, (shift1, scale1), mod_map, pos_tabs, tab_map, wp, dims, tm)
    yc, new_state = conv_state(v)
    at = attend(q, ckv, kpe, k, vv)
    xm, h2, idx, prob, pos, cnt = _mixer(x, yc, at, (shift1, scale1, gate1, shift2, scale2), mod_map, wp, dims, tm)
    if mod[0].shape[1] == 1:
        tmc = 256 if n % 256 == 0 else n
        perc = group // tmc
        comb_map = lambda i: (i // perc, 0, 0)
        g2 = gate2
    else:
        tmc = 256 if n % 256 == 0 else n
        comb_map = lambda i: (i, 0, 0)
        g2 = gate2.reshape(n // tmc, tmc, d)
    y = _moe(xm, h2, idx, prob, pos, cnt, g2, comb_map, *ex, dims["e"])
    return y, ckv, kpe, new_state


def kernel(x_prompt, x_sample, cache_kv_latent, cache_k_rope, state_conv, page_table, c_prompt, c_sample, w_ada, b_ada, g_norm_mix, g_norm_ffn, w_in, w_dw, b_dw, g_conv_ln, b_conv_ln, w_conv_out, g_q_lat, w_uq, g_kv_lat, w_uk, w_uv, g_q_head, g_k_head, w_attn_out, w_out, w_router, b_router, w_up, b_up, w_down, b_down):
    b, s, d = x_prompt.shape
    bs, ts, _ = x_sample.shape
    depth = w_ada.shape[0]
    width, c = w_dw.shape[1], w_dw.shape[2]
    ql, h, qk = w_uq.shape[1], w_uq.shape[2], w_uq.shape[3]
    kl, nope, vd = w_uk.shape[1], w_uk.shape[3], w_uv.shape[3]
    rope = qk - nope
    e = w_router.shape[2]
    n_pages, ps = page_table.shape[1], cache_kv_latent.shape[2]
    past_len = n_pages * ps
    dims = dict(d=d, c=c, ql=ql, kl=kl, h=h, nope=nope, rope=rope, qk=qk, vd=vd, e=e)
    assert qk <= LANE and rope % 2 == 0 and e <= LANE and s >= width - 1
    assert all(v % LANE == 0 for v in (d, c, ql, kl)) and ts * h <= LANE

    n_p, n_s = b * s, bs * ts
    tabs_p = _rope_tables(jnp.arange(s, dtype=I32), nope, rope, qk)
    tabs_s = _rope_tables(jnp.tile(past_len + jnp.arange(ts, dtype=I32), bs), nope, rope, qk)
    tm_s = _tile(n_s, 512)
    ts_pad = -(-ts // (2 * SUBLANE)) * (2 * SUBLANE)

    yp = x_prompt.reshape(n_p, d)
    ys = x_sample.reshape(n_s, d)
    outs = [[] for _ in range(6)]
    for l in range(depth):
        wp = _prepare((g_norm_mix[l], g_norm_ffn[l], w_in[l], g_q_lat[l], w_uq[l], g_kv_lat[l], w_uk[l], w_uv[l],
                       g_q_head[l], g_k_head[l], w_attn_out[l], w_out[l], w_conv_out[l], w_router[l], b_router[l]),
                      dims, ts)
        ex = (w_up[l].astype(BF16), b_up[l], w_down[l].astype(BF16), b_down[l])
        mod = _adaln(jnp.concatenate([c_prompt, c_sample], axis=0), w_ada[l], b_ada[l]).reshape(b + bs, N_MOD, d)
        mod_p = [mod[:b, j].reshape(b, 1, d) for j in range(N_MOD)]
        mod_s = [jnp.repeat(mod[b:, j], ts, axis=0).reshape(n_s // tm_s, tm_s, d) for j in range(N_MOD)]

        def conv_p(v):
            yc, ns = _conv(v.reshape(b, s, c), jnp.zeros((b, width - 1, c), F32), w_dw[l], b_dw[l], g_conv_ln[l],
                           b_conv_ln[l], s)
            return yc.reshape(n_p, c), ns

        def conv_s(v):
            vp = jnp.pad(v.reshape(bs, ts, c), [(0, 0), (0, ts_pad - ts), (0, 0)])
            yc, ns = _conv(vp, state_conv[l], w_dw[l], b_dw[l], g_conv_ln[l], b_conv_ln[l], ts)
            return yc[:, :ts].reshape(n_s, c), ns

        def attend_p(q, ckv, kpe, k, vv):
            return _attention(q, k, vv, b, s, h, vd)

        def attend_s(q, ckv, kpe, k, vv):
            a = _absorb(q, wp["gkh"], wp["wukt"], wp["sel"], h, kl).reshape(bs, ts * h, kl + LANE)
            o = _paged_attention(a, ckv.reshape(bs, ts, kl), kpe.reshape(bs, ts, rope), cache_kv_latent[l],
                                 jnp.swapaxes(cache_k_rope[l], 1, 2), page_table, wp, dims)
            return o.reshape(n_s, h * vd)

        yp, c_p, r_p, s_p = _layer(yp, mod_p, tabs_p, conv_p, attend_p, wp, ex, dims, s)
        ys, c_s, r_s, s_s = _layer(ys, mod_s, tabs_s, conv_s, attend_s, wp, ex, dims, tm_s)
        for lst, val in zip(outs, (c_p.reshape(b, s, kl), r_p.reshape(b, s, rope), s_p, c_s.reshape(bs, ts, kl),
                                   r_s.reshape(bs, ts, rope), s_s)):
            lst.append(val)
    return (yp.reshape(b, s, d), ys.reshape(bs, ts, d)) + tuple(jnp.stack(o) for o in outs)
```

```python
import functools

import jax
import jax.numpy as jnp
from jax import lax
from jax.experimental import pallas as pl
from jax.experimental.pallas import tpu as pltpu

F32 = jnp.float32
BF16 = jnp.bfloat16
I32 = jnp.int32
U32 = jnp.uint32

EPS = 1e-6
TOP_K = 4
ROPE_THETA = 10000.0
SWIGLU_LIMIT = 7.0
SWIGLU_ALPHA = 1.702
N_MOD = 6

LANE = 128
SUBLANE = 8
VMEM_LIMIT_BYTES = 56 * 1024 * 1024
NEG = -1e30


def _cp(*sem):
    return pltpu.CompilerParams(dimension_semantics=sem, vmem_limit_bytes=VMEM_LIMIT_BYTES)


def _tile(n, pref, mult=SUBLANE):
    t = min(pref, n)
    t -= t % mult
    while t >= mult:
        if n % t == 0:
            return t
        t -= mult
    return n


def _sigmoid(x):
    return 1.0 / (1.0 + jnp.exp(-x))


def _rms(x, g):
    return x * lax.rsqrt(jnp.mean(x * x, axis=-1, keepdims=True) + EPS) * g


def _dot(a, b):
    return jnp.dot(a, b, preferred_element_type=F32)


def _store_row_tiles(ref, x):
    r = x.shape[0]
    nc = x.shape[1] // LANE
    for c in range(nc):
        ref[pl.ds(c, r, stride=nc), :] = x[:, c * LANE:(c + 1) * LANE]


def _load_row_tiles(ref, nc):
    r = ref.shape[0] // nc
    return jnp.concatenate([ref[pl.ds(c, r, stride=nc), :] for c in range(nc)], axis=1)


def _adaln_kernel(c_ref, w_ref, b_ref, o_ref):
    c = c_ref[...]
    s = (c * _sigmoid(c)).astype(BF16)
    o_ref[...] = _dot(s, w_ref[...].astype(BF16)) + b_ref[...]


def _adaln(c, w_ada, b_ada):
    r, d = c.shape
    n = w_ada.shape[1]
    tn = _tile(n, 1536, LANE)
    return pl.pallas_call(
        _adaln_kernel,
        out_shape=jax.ShapeDtypeStruct((r, n), F32),
        grid=(n // tn,),
        in_specs=[
            pl.BlockSpec((r, d), lambda j: (0, 0)),
            pl.BlockSpec((d, tn), lambda j: (0, j)),
            pl.BlockSpec((1, tn), lambda j: (0, j)),
        ],
        out_specs=pl.BlockSpec((r, tn), lambda j: (0, j)),
        compiler_params=_cp("arbitrary"),
        name="adaln",
    )(c, w_ada, b_ada.reshape(1, n))


def _inproj_kernel(x_ref, sh_ref, sc_ref, cq_ref, sq_ref, ck_ref, sk_ref, gmix_ref, w1_ref, gql_ref, wq_ref,
                   gqh_ref, gkvl_ref, wuk_ref, epe_ref, gkh_ref, wuv_ref,
                   v_ref, q_ref, ckv_ref, kpe_ref, k_ref, vv_ref, *, c, ql, kl, h, qk, rope):
    x = x_ref[...]
    hmod = _rms(x, gmix_ref[...]) * (1.0 + sc_ref[0]) + sh_ref[0]
    p = _dot(hmod.astype(BF16), w1_ref[...])
    o_q = 2 * c
    o_kv = o_q + ql
    o_kr = o_kv + kl
    qn = _rms(p[:, o_q:o_kv], gql_ref[...])
    qq = _dot(qn.astype(BF16), wq_ref[...])
    ckv = _rms(p[:, o_kv:o_kr], gkvl_ref[...])
    kpe = p[:, o_kr:o_kr + LANE] * ck_ref[...] + p[:, o_kr + LANE:o_kr + 2 * LANE] * sk_ref[...]
    cb = ckv.astype(BF16)
    kf = _dot(cb, wuk_ref[...]) + _dot(kpe.astype(BF16), epe_ref[...])
    vv_ref[...] = _dot(cb, wuv_ref[...]).astype(vv_ref.dtype)
    v_ref[...] = p[:, :c] * _sigmoid(p[:, c:2 * c])
    ckv_ref[...] = ckv
    kpe_ref[...] = kpe[:, :rope]
    cq = cq_ref[...]
    sq = sq_ref[...]
    inv_qk = 1.0 / qk
    for hd in range(h):
        qr = qq[:, hd * LANE:(hd + 1) * LANE] * cq + qq[:, (h + hd) * LANE:(h + hd + 1) * LANE] * sq
        ms = jnp.sum(qr * qr, axis=-1, keepdims=True) * inv_qk
        q_ref[:, hd * LANE:(hd + 1) * LANE] = (qr * lax.rsqrt(ms + EPS) * gqh_ref[...]).astype(q_ref.dtype)
    for hd in range(h):
        kh = kf[:, hd * LANE:(hd + 1) * LANE]
        ms = jnp.sum(kh * kh, axis=-1, keepdims=True) * inv_qk
        k_ref[:, hd * LANE:(hd + 1) * LANE] = (kh * lax.rsqrt(ms + EPS) * gkh_ref[...]).astype(k_ref.dtype)


def _inproj(x, mods, mod_map, tabs, tab_map, wp, dims, tm):
    n, d = x.shape
    c, ql, kl, h, qk, rope, vd = dims["c"], dims["ql"], dims["kl"], dims["h"], dims["qk"], dims["rope"], dims["vd"]
    mblk = (1,) + mods[0].shape[1:]
    full = lambda a: pl.BlockSpec(a.shape, lambda i: (0,) * a.ndim)
    row = lambda w: pl.BlockSpec((tm, w), lambda i: (i, 0))
    tab = pl.BlockSpec((tm, LANE), tab_map)
    weights = [wp["gmix"], wp["w1"], wp["gql"], wp["wq"], wp["gqh"], wp["gkvl"], wp["wuk"], wp["epe"], wp["gkh"],
               wp["wuv"]]
    return pl.pallas_call(
        functools.partial(_inproj_kernel, c=c, ql=ql, kl=kl, h=h, qk=qk, rope=rope),
        out_shape=(
            jax.ShapeDtypeStruct((n, c), F32),
            jax.ShapeDtypeStruct((n, h * LANE), BF16),
            jax.ShapeDtypeStruct((n, kl), F32),
            jax.ShapeDtypeStruct((n, rope), F32),
            jax.ShapeDtypeStruct((n, h * LANE), BF16),
            jax.ShapeDtypeStruct((n, h * vd), BF16),
        ),
        grid=(n // tm,),
        in_specs=[row(d), pl.BlockSpec(mblk, mod_map), pl.BlockSpec(mblk, mod_map), tab, tab, tab, tab]
        + [full(w) for w in weights],
        out_specs=(row(c), row(h * LANE), row(kl), row(rope), row(h * LANE), row(h * vd)),
        compiler_params=_cp("parallel"),
        name="inproj",
    )(x, mods[0], mods[1], *tabs, *weights)


def _conv_kernel(v_ref, st_ref, wdw_ref, gln_ref, bln_ref, y_ref, ns_ref, win_ref, xs_ref, pre_ref, *, width, tt,
                 tv, rb, halo):
    j = pl.program_id(1)
    off = halo - (width - 1)
    nl = win_ref.shape[0]
    for lt in range(nl):
        ls = slice(lt * LANE, (lt + 1) * LANE)

        @pl.when(j == 0)
        def _():
            win_ref[lt, pl.ds(off, width - 1), :] = st_ref[0, :, ls]

        @pl.when(j > 0)
        def _():
            win_ref[lt, pl.ds(off, width - 1), :] = win_ref[lt, pl.ds(tt + off, width - 1), :]

        win_ref[lt, pl.ds(halo, tt), :] = v_ref[0, :, ls]
    gln = gln_ref[...]
    bln = bln_ref[...]
    span = rb + halo - SUBLANE

    def body(i, carry):
        r0 = pl.multiple_of(i * rb, rb)

        def lane_body(lt, inner):
            col = win_ref[lt, pl.ds(r0, rb + halo), :]
            acc = jnp.broadcast_to(wdw_ref[lt, width:width + 1, :], (rb, LANE))
            for s in range(SUBLANE):
                taps = [k for k in range(width) if (off + k) % SUBLANE == s]
                if not taps:
                    continue
                if s > 0:
                    xs_ref[0:span, :] = col[s:s + span, :]
                for k in taps:
                    a0 = (off + k) // SUBLANE * SUBLANE
                    xk = col[a0:a0 + rb, :] if s == 0 else xs_ref[a0:a0 + rb, :]
                    acc = acc + wdw_ref[lt, k:k + 1, :] * xk
            pre_ref[lt, pl.ds(r0, rb), :] = acc
            return inner

        lax.fori_loop(0, nl, lane_body, 0)
        return carry

    lax.fori_loop(0, tt // rb, body, 0)

    def norm_body(i, carry):
        r0 = pl.multiple_of(i * rb, rb)
        acc = jnp.concatenate([pre_ref[lt, pl.ds(r0, rb), :] for lt in range(nl)], axis=-1)
        mu = jnp.mean(acc, axis=-1, keepdims=True)
        dlt = acc - mu
        var = jnp.mean(dlt * dlt, axis=-1, keepdims=True)
        yn = dlt * lax.rsqrt(var + EPS) * gln + bln
        y_ref[0, pl.ds(r0, rb), :] = (yn * _sigmoid(yn)).astype(y_ref.dtype)
        return carry

    lax.fori_loop(0, tt // rb, norm_body, 0, unroll=min(4, tt // rb))

    @pl.when(j == pl.num_programs(1) - 1)
    def _():
        for lt in range(nl):
            ns_ref[0, :, lt * LANE:(lt + 1) * LANE] = win_ref[lt, pl.ds(tv + off, width - 1), :]


def _conv(v, state, w_dw, b_dw, g_ln, b_ln, t_valid):
    b, t, c = v.shape
    width = w_dw.shape[0]
    nl = c // LANE
    halo = -(-(width - 1) // SUBLANE) * SUBLANE
    tt = _tile(t, 512)
    rb = _tile(tt, 64, 2 * SUBLANE)
    nt = t // tt
    tv = t_valid - (nt - 1) * tt
    wp = -(-(width + 1) // SUBLANE) * SUBLANE
    wb = jnp.concatenate([w_dw, b_dw.reshape(1, c), jnp.zeros((wp - width - 1, c), F32)], axis=0)
    wb = wb.reshape(wp, nl, LANE).transpose(1, 0, 2)
    vec = lambda a: a.reshape(1, c)
    cst = lambda shape: pl.BlockSpec(shape, lambda i, j: (0,) * len(shape))
    return pl.pallas_call(
        functools.partial(_conv_kernel, width=width, tt=tt, tv=tv, rb=rb, halo=halo),
        out_shape=(jax.ShapeDtypeStruct((b, t, c), BF16), jax.ShapeDtypeStruct((b, width - 1, c), F32)),
        grid=(b, nt),
        in_specs=[
            pl.BlockSpec((1, tt, c), lambda i, j: (i, j, 0)),
            pl.BlockSpec((1, width - 1, c), lambda i, j: (i, 0, 0)),
            cst((nl, wp, LANE)), cst((1, c)), cst((1, c)),
        ],
        out_specs=(
            pl.BlockSpec((1, tt, c), lambda i, j: (i, j, 0)),
            pl.BlockSpec((1, width - 1, c), lambda i, j: (i, 0, 0)),
        ),
        scratch_shapes=[pltpu.VMEM((nl, tt + halo, LANE), F32), pltpu.VMEM((rb + halo, LANE), F32),
                        pltpu.VMEM((nl, tt, LANE), F32)],
        compiler_params=_cp("parallel", "arbitrary"),
        name="conv",
    )(v, state, wb, vec(g_ln), vec(b_ln))


def _attn_kernel(q_ref, k_ref, v_ref, o_ref, *, tq, tk, vd):
    qi = pl.program_id(2)
    q = q_ref[0]
    per = tq // tk
    krow = lax.broadcasted_iota(I32, (tk, tq), 0)
    qcol = lax.broadcasted_iota(I32, (tk, tq), 1)

    def block(j, carry, diag):
        ks = pl.multiple_of(j * tk, tk)
        kb = k_ref[0, pl.ds(ks, tk), :]
        vb = v_ref[0, pl.ds(ks, tk), :]
        sts = [lax.dot_general(kb[:, hh * LANE:(hh + 1) * LANE], q[:, hh * LANE:(hh + 1) * LANE],
                               (((1,), (1,)), ((), ())), preferred_element_type=F32) for hh in range(2)]
        out = []
        for hh in range(2):
            m, l, acc = carry[hh]
            st = sts[hh]
            if diag is not None:
                st = jnp.where(krow + diag * tk <= qcol, st, NEG)
            m_new = jnp.maximum(m, jnp.max(st, axis=0, keepdims=True))
            alpha = jnp.exp(m - m_new)
            pt = jnp.exp(st - m_new)
            l = alpha * l + jnp.sum(pt, axis=0, keepdims=True)
            pv = lax.dot_general(vb, pt.astype(BF16), (((0,), (0,)), ((), ())), preferred_element_type=F32)
            out.append((m_new, l, alpha * acc + pv))
        return tuple(out)

    one = (jnp.full((1, tq), NEG, F32), jnp.zeros((1, tq), F32), jnp.zeros((LANE, tq), F32))
    carry = lax.fori_loop(0, qi * per, lambda j, cr: block(j, cr, None), (one, one))
    for u in range(per):
        carry = block(qi * per + u, carry, u)
    o0 = carry[0][2] / carry[0][1]
    o1 = carry[1][2] / carry[1][1]
    vrow = lax.broadcasted_iota(I32, (LANE, tq), 0)
    o_ref[0] = jnp.where(vrow < vd, o0, o1).T.astype(o_ref.dtype)


def _attention(q, k, v, b, s, h, vd):
    assert 2 * vd == LANE and h % 2 == 0
    tq = _tile(s, 512)
    tk = _tile(tq, 512)
    q3 = q.reshape(b, s, h * LANE)
    k3 = k.reshape(b, s, h * LANE)
    v3 = v.reshape(b, s, h * vd)
    out = pl.pallas_call(
        functools.partial(_attn_kernel, tq=tq, tk=tk, vd=vd),
        out_shape=jax.ShapeDtypeStruct((b, s, h * vd), BF16),
        grid=(b, h // 2, s // tq),
        in_specs=[
            pl.BlockSpec((1, tq, 2 * LANE), lambda i, p, j: (i, j, p)),
            pl.BlockSpec((1, s, 2 * LANE), lambda i, p, j: (i, 0, p)),
            pl.BlockSpec((1, s, LANE), lambda i, p, j: (i, 0, p)),
        ],
        out_specs=pl.BlockSpec((1, tq, LANE), lambda i, p, j: (i, j, p)),
        compiler_params=_cp("parallel", "parallel", "arbitrary"),
        name="attention",
    )(q3, k3, v3)
    return out.reshape(b * s, h * vd)


def _absorb_kernel(q_ref, gkh_ref, wukt_ref, sel_ref, a_ref, *, h, kl):
    aw = kl + LANE
    for hd in range(h):
        qg = (q_ref[:, hd * LANE:(hd + 1) * LANE].astype(F32) * gkh_ref[...]).astype(BF16)
        a_ref[:, hd * aw:hd * aw + kl] = _dot(qg, wukt_ref[hd])
        a_ref[:, hd * aw + kl:(hd + 1) * aw] = _dot(qg, sel_ref[...])


def _absorb(q, gkh_pad, wukt, sel, h, kl):
    n = q.shape[0]
    full = lambda a: pl.BlockSpec(a.shape, lambda i: (0,) * a.ndim)
    return pl.pallas_call(
        functools.partial(_absorb_kernel, h=h, kl=kl),
        out_shape=jax.ShapeDtypeStruct((n, h * (kl + LANE)), F32),
        grid=(1,),
        in_specs=[full(q), full(gkh_pad), full(wukt), full(sel)],
        out_specs=pl.BlockSpec((n, h * (kl + LANE)), lambda i: (0, 0)),
        compiler_params=_cp("arbitrary"),
        name="absorb",
    )(q, gkh_pad, wukt, sel)


def _paged_kernel(pt_ref, a_ref, cn_ref, pn_ref, wukt_ref, sel_ref, wuv_ref, poolc_ref, poolp_ref, o_ref,
                  cbuf, pbuf, sems, lhs_sc, m_sc, l_sc, acc_sc, *, pg, kl, rope, qk, h, nope, vd, t):
    j = pl.program_id(1)
    nj = pl.num_programs(1)
    step = pl.program_id(0) * nj + j
    last = pl.num_programs(0) * nj - 1
    slot = step % 2

    def page_copies(stp, sl):
        out = []
        for u in range(pg):
            page = pt_ref[stp * pg + u]
            out.append(pltpu.make_async_copy(poolc_ref.at[page], cbuf.at[sl, u], sems.at[0, sl]))
            out.append(pltpu.make_async_copy(poolp_ref.at[page], pbuf.at[sl, u], sems.at[1, sl]))
        return out

    def start_all(copies):
        for n_cp, cp in enumerate(copies):
            cp.start(priority=(n_cp // 2) % 2)

    @pl.when(step == 0)
    def _():
        start_all(page_copies(0, 0))

    start_all(page_copies(jnp.minimum(step + 1, last), 1 - slot))
    pltpu.make_async_copy(poolc_ref.at[pl.ds(0, pg)], cbuf.at[slot], sems.at[0, slot]).wait()
    pltpu.make_async_copy(poolp_ref.at[pl.ds(0, pg)], pbuf.at[slot], sems.at[1, slot]).wait()
    th = t * h
    hn = h * nope
    a = a_ref[0]
    a_p = a[:, kl:kl + rope].astype(BF16)
    inv_qk = 1.0 / qk

    @pl.when(j == 0)
    def _():
        lhs_sc[0:hn, :] = wukt_ref[...]
        lhs_sc[hn:hn + th, :] = a[:, :kl].astype(BF16)
        m_sc[...] = jnp.full(m_sc.shape, NEG, F32)
        l_sc[...] = jnp.zeros(l_sc.shape, F32)
        acc_sc[...] = jnp.zeros(acc_sc.shape, F32)

    def fold(x):
        out = x[0:SUBLANE]
        for r in range(1, x.shape[0] // SUBLANE):
            out = out + x[r * SUBLANE:(r + 1) * SUBLANE]
        return out

    def project(cb):
        return lax.dot_general(lhs_sc[...], cb, (((1,), (1,)), ((), ())), preferred_element_type=F32)

    def scores(res, num_rope, rope_sq):
        parts = []
        for hd in range(h):
            blk = res[hd * nope:(hd + 1) * nope, :]
            parts.append(fold(blk * blk))
        parts.append(rope_sq)
        parts.append(jnp.zeros_like(parts[0]))
        stack = jnp.concatenate(parts, axis=0)
        hi = stack.astype(BF16)
        lo = (stack - hi.astype(F32)).astype(BF16)
        ksq = _dot(sel_ref[...], hi) + _dot(sel_ref[...], lo)
        rinv = lax.rsqrt(ksq * inv_qk + EPS)
        num = res[hn:hn + th, :] + num_rope
        return jnp.concatenate([num[tt * h:(tt + 1) * h, :] * rinv for tt in range(t)], axis=0)

    def update(cf, ptf, valid):
        num_rope = _dot(a_p, ptf.astype(BF16))
        rope_sq = fold(ptf * ptf)
        cb = cf.astype(BF16)
        s = scores(project(cb), num_rope, rope_sq)
        if valid is not None:
            s = jnp.where(valid, s, NEG)
        m_old = m_sc[:, 0:1]
        m_new = jnp.maximum(m_old, jnp.max(s, axis=-1, keepdims=True))
        alpha = jnp.exp(m_old - m_new)
        p = jnp.exp(s - m_new)
        l_sc[...] = alpha * l_sc[...] + jnp.sum(p, axis=-1, keepdims=True)
        m_sc[...] = jnp.broadcast_to(m_new, m_sc.shape)
        acc_sc[...] = alpha * acc_sc[...] + _dot(p.astype(BF16), cb)

    ps = cbuf.shape[2]
    cf = cbuf[slot].reshape(pg * ps, kl)
    ptf = jnp.concatenate([pbuf[slot, u] for u in range(pg)], axis=1)
    update(cf, ptf, None)

    @pl.when(step == last)
    def _():
        pltpu.make_async_copy(poolc_ref.at[pl.ds(0, pg)], cbuf.at[1 - slot], sems.at[0, 1 - slot]).wait()
        pltpu.make_async_copy(poolp_ref.at[pl.ds(0, pg)], pbuf.at[1 - slot], sems.at[1, 1 - slot]).wait()

    @pl.when(j == nj - 1)
    def _():
        tn = cn_ref.shape[1]
        kcol = lax.broadcasted_iota(I32, (th, tn), 1)
        qrow = lax.broadcasted_iota(I32, (th, tn), 0) // h
        update(cn_ref[0], pn_ref[0], jnp.logical_and(kcol <= qrow, kcol < t))
        on = (acc_sc[...] / l_sc[:, 0:1]).astype(BF16)
        oa = _dot(on, wuv_ref[...])
        r_h = lax.broadcasted_iota(I32, (th, h * vd), 0) % h
        c_h = lax.broadcasted_iota(I32, (th, h * vd), 1) // vd
        oa = jnp.where(r_h == c_h, oa, 0.0)
        rows = [jnp.sum(oa[tt * h:(tt + 1) * h, :], axis=0, keepdims=True) for tt in range(t)]
        o_ref[0] = jnp.concatenate(rows, axis=0)


def _paged_attention(a, c_new, pe_new, pool_c, pool_pet, page_table, wp, dims):
    bs, th, aw = a.shape
    t = c_new.shape[1]
    kl, rope, qk, h, vd, nope = dims["kl"], dims["rope"], dims["qk"], dims["h"], dims["vd"], dims["nope"]
    assert h == SUBLANE and nope % SUBLANE == 0 and rope % SUBLANE == 0 and t <= LANE
    c_new = jnp.pad(c_new, [(0, 0), (0, LANE - t), (0, 0)])
    pet_new = jnp.pad(jnp.swapaxes(pe_new, 1, 2), [(0, 0), (0, 0), (0, LANE - t)])
    n_pages = page_table.shape[1]
    ps = pool_c.shape[1]
    pg = _tile(n_pages, 32, 1)
    pt = page_table.reshape(-1).astype(I32)

    cst = lambda arr: pl.BlockSpec(arr.shape, lambda i, j, ptr: (0,) * arr.ndim)
    in_specs = [
        pl.BlockSpec((1, th, aw), lambda i, j, ptr: (i, 0, 0)),
        pl.BlockSpec((1, LANE, kl), lambda i, j, ptr: (i, 0, 0)),
        pl.BlockSpec((1, rope, LANE), lambda i, j, ptr: (i, 0, 0)),
        cst(wp["wukt_flat"]), cst(wp["sel_sq"]), cst(wp["wuv"]),
        pl.BlockSpec(memory_space=pl.ANY), pl.BlockSpec(memory_space=pl.ANY),
    ]
    grid_spec = pltpu.PrefetchScalarGridSpec(
        num_scalar_prefetch=1,
        grid=(bs, n_pages // pg),
        in_specs=in_specs,
        out_specs=pl.BlockSpec((1, t, h * vd), lambda i, j, ptr: (i, 0, 0)),
        scratch_shapes=[pltpu.VMEM((2, pg, ps, kl), F32), pltpu.VMEM((2, pg, rope, ps), F32),
                        pltpu.SemaphoreType.DMA((2, 2)),
                        pltpu.VMEM((h * nope + th, kl), BF16), pltpu.VMEM((th, LANE), F32),
                        pltpu.VMEM((th, LANE), F32), pltpu.VMEM((th, kl), F32)],
    )
    return pl.pallas_call(
        functools.partial(_paged_kernel, pg=pg, kl=kl, rope=rope, qk=qk, h=h, nope=nope, vd=vd, t=t),
        out_shape=jax.ShapeDtypeStruct((bs, t, h * vd), F32),
        grid_spec=grid_spec,
        compiler_params=_cp("arbitrary", "arbitrary"),
        name="paged_attention",
    )(pt, a, c_new, pet_new, wp["wukt_flat"], wp["sel_sq"], wp["wuv"], pool_c, pool_pet)


def _mixer_kernel(x_ref, yc_ref, at_ref, sh1_ref, sc1_ref, g1_ref, sh2_ref, sc2_ref, gmix_ref, wg_ref, wco_ref,
                  wao_ref, wout_ref, gffn_ref, wrh_ref, wrl_ref, br_ref, tri_ref,
                  xm_ref, h2_ref, idx_ref, prob_ref, pos_ref, cnt_ref, run_ref, *, d, e):
    conv_out = _dot(yc_ref[...].astype(BF16), wco_ref[...])
    attn_out = _dot(at_ref[...].astype(BF16), wao_ref[...])
    x = x_ref[...]
    hmod = _rms(x, gmix_ref[...]) * (1.0 + sc1_ref[0]) + sh1_ref[0]
    gates = _dot(hmod.astype(BF16), wg_ref[...])
    merged = _sigmoid(gates[:, :d]) * conv_out + _sigmoid(gates[:, d:]) * attn_out
    xm = x + g1_ref[0] * _dot(merged.astype(BF16), wout_ref[...])
    xm_ref[...] = xm
    h2 = _rms(xm, gffn_ref[...]) * (1.0 + sc2_ref[0]) + sh2_ref[0]
    _store_row_tiles(h2_ref, h2)
    hi = h2.astype(BF16)
    lo = (h2 - hi.astype(F32)).astype(BF16)
    lg = _dot(hi, wrh_ref[...]) + _dot(lo, wrh_ref[...]) + _dot(hi, wrl_ref[...]) + br_ref[...]
    tm = x.shape[0]
    lane = lax.broadcasted_iota(I32, (tm, LANE), 1)
    lane_f = lane.astype(F32)
    lg = jnp.where(lane < e, lg, -jnp.inf)
    vals, idxs = [], []
    for _ in range(TOP_K):
        m = jnp.max(lg, axis=-1, keepdims=True)
        ik = jnp.min(jnp.where(lg == m, lane_f, float(LANE)), axis=-1, keepdims=True)
        vals.append(m)
        idxs.append(ik)
        lg = jnp.where(lane_f == ik, -jnp.inf, lg)
    ex = [jnp.exp(vk - vals[0]) for vk in vals]
    den = ex[0]
    for ek in ex[1:]:
        den = den + ek
    @pl.when(pl.program_id(0) == 0)
    def _():
        run_ref[...] = jnp.zeros(run_ref.shape, F32)

    ohs = [lane_f == ik for ik in idxs]
    sel = ohs[0].astype(F32)
    for oh in ohs[1:]:
        sel = sel + oh.astype(F32)
    before = _dot(tri_ref[...], sel.astype(BF16)) + run_ref[0:1, :]
    idx_o = jnp.zeros((tm, LANE), F32)
    prob_o = jnp.zeros((tm, LANE), F32)
    pos_o = jnp.zeros((tm, LANE), F32)
    for k in range(TOP_K):
        idx_o = jnp.where(lane == k, idxs[k], idx_o)
        prob_o = jnp.where(lane == k, ex[k] / den, prob_o)
        pos_o = jnp.where(lane == k, jnp.sum(jnp.where(ohs[k], before, 0.0), axis=-1, keepdims=True), pos_o)
    idx_ref[...] = idx_o.astype(I32)
    prob_ref[...] = prob_o
    pos_ref[...] = pos_o.astype(I32)
    run_ref[...] = run_ref[...] + jnp.sum(sel, axis=0, keepdims=True)
    cnt_ref[...] = run_ref[...]


def _mixer(x, yc, at, mods, mod_map, wp, dims, tm):
    n, d = x.shape
    e = dims["e"]
    tri = (jnp.arange(tm)[None, :] < jnp.arange(tm)[:, None]).astype(BF16)
    mblk = (1,) + mods[0].shape[1:]
    full = lambda a: pl.BlockSpec(a.shape, lambda i: (0,) * a.ndim)
    row = lambda w: pl.BlockSpec((tm, w), lambda i: (i, 0))
    mspec = pl.BlockSpec(mblk, mod_map)
    weights = [wp["gmix"], wp["wg"], wp["wco"], wp["wao"], wp["wout"], wp["gffn"], wp["wrh"], wp["wrl"], wp["br"],
               tri]
    return pl.pallas_call(
        functools.partial(_mixer_kernel, d=d, e=e),
        out_shape=(
            jax.ShapeDtypeStruct((n, d), F32),
            jax.ShapeDtypeStruct((n * (d // LANE), LANE), F32),
            jax.ShapeDtypeStruct((n, LANE), I32),
            jax.ShapeDtypeStruct((n, LANE), F32),
            jax.ShapeDtypeStruct((n, LANE), I32),
            jax.ShapeDtypeStruct((SUBLANE, LANE), F32),
        ),
        grid=(n // tm,),
        in_specs=[row(d), row(yc.shape[1]), row(at.shape[1]), mspec, mspec, mspec, mspec, mspec]
        + [full(w) for w in weights],
        out_specs=(row(d), pl.BlockSpec((tm * (d // LANE), LANE), lambda i: (i, 0)), row(LANE), row(LANE),
                   row(LANE), pl.BlockSpec((SUBLANE, LANE), lambda i: (0, 0))),
        scratch_shapes=[pltpu.VMEM((SUBLANE, LANE), F32)],
        compiler_params=_cp("arbitrary"),
        name="mixer",
    )(x, yc, at, mods[0], mods[1], mods[2], mods[3], mods[4], *weights)


def _lane_cumsum(x):
    lane = lax.broadcasted_iota(I32, x.shape, 1)
    s = 1
    while s < LANE:
        x = x + jnp.where(lane >= s, pltpu.roll(x, s, axis=1), 0.0)
        s *= 2
    return x


def _route_kernel(idx_ref, pos_ref, cnt_ref, dest_ref, meta_ref, *, e, blk, nbp, tm):
    i = pl.program_id(0)
    cnt = cnt_ref[...]
    gend = _lane_cumsum(cnt)
    gstart = gend - cnt

    @pl.when(i == 0)
    def _():
        inv = 1.0 / blk
        first_b = jnp.floor(gstart * inv)
        last_b = jnp.floor((gend - 1.0) * inv)
        n_it = jnp.where(cnt > 0.0, last_b - first_b + 1.0, 0.0)
        it_end = _lane_cumsum(n_it)
        it_off = it_end - n_it
        lane_w = lax.broadcasted_iota(I32, (nbp, LANE), 1)
        w = lax.broadcasted_iota(I32, (nbp, LANE), 0).astype(F32)[:, 0:1]
        ex = jnp.sum(jnp.where(jnp.logical_and(lane_w < e, it_end[0:1, :] <= w), 1.0, 0.0), axis=-1, keepdims=True)
        ex = jnp.minimum(ex, float(e - 1))
        ohw = lane_w.astype(F32) == ex
        pick = lambda v: jnp.sum(jnp.where(ohw, v[0:1, :], 0.0), axis=-1, keepdims=True)
        bk = pick(first_b) + (w - pick(it_off))
        lo = jnp.maximum(pick(gstart), bk * blk) - bk * blk
        hi = jnp.minimum(pick(gend), (bk + 1.0) * blk) - bk * blk
        n_items = jnp.max(it_end[0:1, :], axis=-1, keepdims=True)
        meta = jnp.zeros((nbp, LANE), F32)
        for col, val in enumerate((ex, bk, lo, hi, n_items)):
            meta = jnp.where(lane_w == col, val, meta)
        meta_ref[...] = meta.astype(I32)

    lane = lax.broadcasted_iota(I32, (tm, LANE), 1)
    idx = idx_ref[...]
    pos = pos_ref[...].astype(F32)
    dest = jnp.zeros((tm, LANE), F32)
    for k in range(TOP_K):
        gk = jnp.sum(jnp.where(lane == idx[:, k:k + 1], gstart[0:1, :], 0.0), axis=-1, keepdims=True)
        dest = jnp.where(lane == k, gk + pos[:, k:k + 1], dest)
    dest_ref[...] = dest.astype(I32)


def _route(idx, pos, cnt, e, blk, nbp):
    n = idx.shape[0]
    tm = _tile(n, 1024)
    return pl.pallas_call(
        functools.partial(_route_kernel, e=e, blk=blk, nbp=nbp, tm=tm),
        out_shape=(jax.ShapeDtypeStruct((n, LANE), I32), jax.ShapeDtypeStruct((nbp, LANE), I32)),
        grid=(n // tm,),
        in_specs=[pl.BlockSpec((tm, LANE), lambda i: (i, 0)), pl.BlockSpec((tm, LANE), lambda i: (i, 0)),
                  pl.BlockSpec((SUBLANE, LANE), lambda i: (0, 0))],
        out_specs=(pl.BlockSpec((tm, LANE), lambda i: (i, 0)), pl.BlockSpec((nbp, LANE), lambda i: (0, 0))),
        compiler_params=_cp("arbitrary"),
        name="route",
    )(idx, pos, cnt)


def _dispatch_kernel(dest_ref, h_ref, buf_ref, sem, *, tm, nc):
    def body(g, carry):
        t0 = pl.multiple_of(g * SUBLANE, SUBLANE)
        for u in range(SUBLANE):
            for k in range(TOP_K):
                dst = dest_ref[t0 * TOP_K + (u * TOP_K + k)]
                src = pl.ds(pl.multiple_of((t0 + u) * nc, nc), nc)
                pltpu.make_async_copy(h_ref.at[src, :], buf_ref.at[pl.ds(pl.multiple_of(dst * nc, nc), nc), :],
                                      sem).start(priority=k % 2)
        return carry

    lax.fori_loop(0, tm // SUBLANE, body, 0)
    for k in range(TOP_K):
        pltpu.make_async_copy(h_ref, buf_ref.at[pl.ds(0, tm * nc), :], sem).wait()


def _dispatch(h2, dest_flat, rows, nc):
    n = h2.shape[0] // nc
    tm = 256 if n % 256 == 0 else n
    return pl.pallas_call(
        functools.partial(_dispatch_kernel, tm=tm, nc=nc),
        out_shape=jax.ShapeDtypeStruct((rows * nc, LANE), h2.dtype),
        grid=(n // tm,),
        in_specs=[pl.BlockSpec((tm * TOP_K,), lambda i: (i,), memory_space=pltpu.SMEM),
                  pl.BlockSpec((tm * nc, LANE), lambda i: (i, 0))],
        out_specs=pl.BlockSpec(memory_space=pl.ANY),
        scratch_shapes=[pltpu.SemaphoreType.DMA(())],
        compiler_params=_cp("arbitrary"),
        name="dispatch",
    )(dest_flat, h2)


def _expert_kernel(ex_ref, bk_ref, lo_ref, hi_ref, ni_ref, x_ref, wup_ref, bup_ref, wdn_ref, bdn_ref, y_ref,
                   yacc_ref, *, blk, f):
    w = pl.program_id(0)
    del ex_ref

    @pl.when(w == 0)
    def _():
        yacc_ref[...] = jnp.zeros(yacc_ref.shape, F32)

    @pl.when(w < ni_ref[0])
    def _():
        rowi = lax.broadcasted_iota(I32, (blk, 1), 0)
        inside = jnp.logical_and(rowi >= lo_ref[w], rowi < hi_ref[w])
        nc = wup_ref.shape[1] // LANE
        x = jnp.where(inside, _load_row_tiles(x_ref, nc), 0.0).astype(BF16)
        gu = _dot(x, wup_ref[0]) + bup_ref[0]
        gate = jnp.minimum(gu[:, :f], SWIGLU_LIMIT)
        lin = jnp.clip(gu[:, f:], -SWIGLU_LIMIT, SWIGLU_LIMIT)
        act = gate * _sigmoid(SWIGLU_ALPHA * gate) * (lin + 1.0)
        y = _dot(act.astype(BF16), wdn_ref[0]) + bdn_ref[0]
        first = jnp.logical_or(w == 0, bk_ref[w] != bk_ref[jnp.maximum(w - 1, 0)])
        y = jnp.where(inside, y, jnp.where(first, 0.0, yacc_ref[...]))
        yacc_ref[...] = y
        _store_row_tiles(y_ref, y)


def _experts(buf, meta, w_up, b_up, w_dn, b_dn, blk, n_items_max):
    e, d, f2 = w_up.shape
    nc = d // LANE
    f = f2 // 2
    ex, bk, lo, hi = (meta[:, c] for c in range(4))
    ni = meta[:1, 4]

    def item(w, ni_r):
        return jnp.minimum(w, ni_r[0] - 1)

    def blk_map(w, ex_r, bk_r, lo_r, hi_r, ni_r):
        return (bk_r[item(w, ni_r)], 0)

    def exp_map(w, ex_r, bk_r, lo_r, hi_r, ni_r):
        return (ex_r[item(w, ni_r)], 0, 0)

    grid_spec = pltpu.PrefetchScalarGridSpec(
        num_scalar_prefetch=5,
        grid=(n_items_max,),
        in_specs=[
            pl.BlockSpec((blk * nc, LANE), blk_map),
            pl.BlockSpec((1, d, f2), exp_map),
            pl.BlockSpec((1, 1, f2), exp_map),
            pl.BlockSpec((1, f, d), exp_map),
            pl.BlockSpec((1, 1, d), exp_map),
        ],
        out_specs=pl.BlockSpec((blk * nc, LANE), blk_map),
        scratch_shapes=[pltpu.VMEM((blk, d), F32)],
    )
    return pl.pallas_call(
        functools.partial(_expert_kernel, blk=blk, f=f),
        out_shape=jax.ShapeDtypeStruct(buf.shape, F32),
        grid_spec=grid_spec,
        compiler_params=_cp("arbitrary"),
        name="experts",
    )(ex, bk, lo, hi, ni, buf, w_up, b_up.reshape(e, 1, f2), w_dn, b_dn.reshape(e, 1, d))


def _combine_kernel(dcur_ref, dnxt_ref, x_ref, g2_ref, p_ref, y_ref, o_ref, yg_ref, sem, *, tm, nc):
    i = pl.program_id(0)
    n = pl.num_programs(0)
    slot = i % 2

    def issue(dref, s):
        def body(g, carry):
            t0 = pl.multiple_of(g * SUBLANE, SUBLANE)
            for u in range(SUBLANE):
                for k in range(TOP_K):
                    src = dref[t0 * TOP_K + (u * TOP_K + k)]
                    pltpu.make_async_copy(y_ref.at[pl.ds(pl.multiple_of(src * nc, nc), nc), :],
                                          yg_ref.at[s, k, pl.ds(pl.multiple_of((t0 + u) * nc, nc), nc), :],
                                          sem.at[s]).start(priority=k % 2)
            return carry

        lax.fori_loop(0, tm // SUBLANE, body, 0)

    @pl.when(i == 0)
    def _():
        issue(dcur_ref, 0)

    @pl.when(i + 1 < n)
    def _():
        issue(dnxt_ref, 1 - slot)

    for k in range(TOP_K):
        pltpu.make_async_copy(y_ref.at[pl.ds(0, tm * nc), :], yg_ref.at[slot, k], sem.at[slot]).wait()
    prob = p_ref[...]
    acc = prob[:, 0:1] * _load_row_tiles(yg_ref.at[slot, 0], nc)
    for k in range(1, TOP_K):
        acc = acc + prob[:, k:k + 1] * _load_row_tiles(yg_ref.at[slot, k], nc)
    o_ref[...] = x_ref[...] + g2_ref[0] * acc


def _combine(xm, g2, mod_map, prob, ybuf, dest_flat):
    n, d = xm.shape
    nc = d // LANE
    tm = 256 if n % 256 == 0 else n
    nt = n // tm
    mblk = (1,) + g2.shape[1:]
    return pl.pallas_call(
        functools.partial(_combine_kernel, tm=tm, nc=nc),
        out_shape=jax.ShapeDtypeStruct((n, d), F32),
        grid=(nt,),
        in_specs=[
            pl.BlockSpec((tm * TOP_K,), lambda i: (i,), memory_space=pltpu.SMEM),
            pl.BlockSpec((tm * TOP_K,), lambda i: (jnp.minimum(i + 1, nt - 1),), memory_space=pltpu.SMEM),
            pl.BlockSpec((tm, d), lambda i: (i, 0)),
            pl.BlockSpec(mblk, mod_map),
            pl.BlockSpec((tm, LANE), lambda i: (i, 0)),
            pl.BlockSpec(memory_space=pl.ANY),
        ],
        out_specs=pl.BlockSpec((tm, d), lambda i: (i, 0)),
        scratch_shapes=[pltpu.VMEM((2, TOP_K, tm * nc, LANE), F32), pltpu.SemaphoreType.DMA((2,))],
        compiler_params=_cp("arbitrary"),
        name="combine",
    )(dest_flat, dest_flat, xm, g2, prob, ybuf)


def _rope_tables(pos, nope, rope, qk):
    p = pos.shape[0]
    inv_freq = ROPE_THETA ** (-jnp.arange(0, rope, 2, dtype=F32) / rope)
    ang = pos.astype(F32)[:, None] * inv_freq
    c2 = jnp.concatenate([jnp.cos(ang)] * 2, axis=-1)
    s2 = jnp.concatenate([jnp.sin(ang)] * 2, axis=-1)
    cq = jnp.concatenate([jnp.ones((p, nope), F32), c2, jnp.ones((p, LANE - qk), F32)], axis=-1)
    sq = jnp.concatenate([jnp.zeros((p, nope), F32), s2, jnp.zeros((p, LANE - qk), F32)], axis=-1)
    ck = jnp.concatenate([c2, jnp.zeros((p, LANE - rope), F32)], axis=-1)
    sk = jnp.concatenate([s2, jnp.zeros((p, LANE - rope), F32)], axis=-1)
    return cq, sq, ck, sk


def _swap_halves(w):
    half = w.shape[-1] // 2
    return jnp.concatenate([-w[..., half:], w[..., :half]], axis=-1)


def _pad_last(w, n):
    return jnp.pad(w, [(0, 0)] * (w.ndim - 1) + [(0, n - w.shape[-1])])


def _prepare(p, dims, t_s):
    (g_mix, g_ffn, w_in, g_ql, w_uq, g_kvl, w_uk, w_uv, g_qh, g_kh, w_ao, w_out, w_co, w_r, b_r) = p
    d, c, ql, kl, h = dims["d"], dims["c"], dims["ql"], dims["kl"], dims["h"]
    nope, rope, qk, vd, e = dims["nope"], dims["rope"], dims["qk"], dims["vd"], dims["e"]
    o_q = 2 * c
    o_kv = o_q + ql
    o_kr = o_kv + kl
    o_ga = o_kr + rope
    w_kr = w_in[:, o_kr:o_ga]
    w1 = jnp.concatenate([w_in[:, :o_kr], _pad_last(w_kr, LANE), _pad_last(_swap_halves(w_kr), LANE)], axis=1)
    wq_plain = _pad_last(w_uq, LANE)
    wq_swap = jnp.concatenate([jnp.zeros((ql, h, nope), F32), _swap_halves(w_uq[..., nope:]),
                               jnp.zeros((ql, h, LANE - qk), F32)], axis=-1)
    wq = jnp.concatenate([wq_plain.reshape(ql, h * LANE), wq_swap.reshape(ql, h * LANE)], axis=1)
    sm_scale = qk ** -0.5
    place = jnp.zeros((LANE, LANE), F32).at[jnp.arange(rope), nope + jnp.arange(rope)].set(1.0)
    epe = jnp.tile(place, (1, h))
    wukt = _pad_last(jnp.transpose(w_uk, (1, 2, 0)), kl)
    wukt = jnp.pad(wukt, [(0, 0), (0, LANE - nope), (0, 0)])
    sel = jnp.zeros((LANE, LANE), F32).at[nope + jnp.arange(rope), jnp.arange(rope)].set(1.0)
    col = jnp.arange((h + 2) * SUBLANE) // SUBLANE
    sel_sq = jnp.logical_or(col[None, :] == jnp.arange(h)[:, None], col[None, :] == h).astype(BF16)
    hi = w_r.astype(BF16)
    lo = (w_r - hi.astype(F32)).astype(BF16)
    row = lambda g: g.reshape(1, -1).astype(F32)
    return dict(
        gmix=row(g_mix), gffn=row(g_ffn), w1=w1.astype(BF16), gql=row(g_ql), wq=wq.astype(BF16),
        gqh=row(_pad_last(g_qh, LANE)) * sm_scale, gkvl=row(g_kvl),
        wuk=_pad_last(w_uk, LANE).reshape(kl, h * LANE).astype(BF16), epe=epe.astype(BF16),
        gkh=row(_pad_last(g_kh, LANE)), wuv=w_uv.reshape(kl, h * vd).astype(BF16),
        wukt=wukt.astype(BF16), sel=sel.astype(BF16), wukt_flat=w_uk.reshape(kl, h * nope).T.astype(BF16),
        sel_sq=sel_sq,
        wg=w_in[:, o_ga:].astype(BF16), wco=w_co.astype(BF16), wao=w_ao.astype(BF16), wout=w_out.astype(BF16),
        wrh=_pad_last(hi, LANE), wrl=_pad_last(lo, LANE), br=row(_pad_last(b_r, LANE)),
    )


def _moe(xm, h2, idx, prob, pos, cnt, g2, mod_map, w_up, b_up, w_dn, b_dn, e):
    n = xm.shape[0]
    n_asg = n * TOP_K
    blk = 512 if n_asg >= 2048 * e else 128
    assert n_asg % blk == 0
    n_items_max = n_asg // blk + e - 1
    nbp = -(-n_items_max // SUBLANE) * SUBLANE
    dest, meta = _route(idx, pos, cnt, e, blk, nbp)
    dest_flat = dest[:, :TOP_K].reshape(-1)
    buf = _dispatch(h2, dest_flat, n_asg, xm.shape[1] // LANE)
    ybuf = _experts(buf, meta, w_up, b_up, w_dn, b_dn, blk, n_items_max)
    return _combine(xm, g2, mod_map, prob, ybuf, dest_flat)


def _layer(x, mod, pos_tabs, conv_state, attend, wp, ex, dims, group):
    n, d = x.shape
    tm = _tile(n, 512)
    if mod[0].shape[1] == 1:
        per = group // tm
        mod_map = lambda i: (i // per, 0, 0)
        tab_map = lambda i: (i % per, 0)
    else:
        mod_map = lambda i: (i, 0, 0)
        tab_map = lambda i: (i, 0)
    shift1, scale1, gate1, shift2, scale2, gate2 = mod
    v, q, ckv, kpe, k, vv = _inproj(x, (shift1, scale1), mod_map, pos_tabs, tab_map, wp, dims, tm)
    yc, new_state = conv_state(v)
    at = attend(q, ckv, kpe, k, vv)
    xm, h2, idx, prob, pos, cnt = _mixer(x, yc, at, (shift1, scale1, gate1, shift2, scale2), mod_map, wp, dims, tm)
    if mod[0].shape[1] == 1:
        tmc = 256 if n % 256 == 0 else n
        perc = group // tmc
        comb_map = lambda i: (i // perc, 0, 0)
        g2 = gate2
    else:
        tmc = 256 if n % 256 == 0 else n
        comb_map = lambda i: (i, 0, 0)
        g2 = gate2.reshape(n // tmc, tmc, d)
    y = _moe(xm, h2, idx, prob, pos, cnt, g2, comb_map, *ex, dims["e"])
    return y, ckv, kpe, new_state


def kernel(x_prompt, x_sample, cache_kv_latent, cache_k_rope, state_conv, page_table, c_prompt, c_sample, w_ada, b_ada, g_norm_mix, g_norm_ffn, w_in, w_dw, b_dw, g_conv_ln, b_conv_ln, w_conv_out, g_q_lat, w_uq, g_kv_lat, w_uk, w_uv, g_q_head, g_k_head, w_attn_out, w_out, w_router, b_router, w_up, b_up, w_down, b_down):
    b, s, d = x_prompt.shape
    bs, ts, _ = x_sample.shape
    depth = w_ada.shape[0]
    width, c = w_dw.shape[1], w_dw.shape[2]
    ql, h, qk = w_uq.shape[1], w_uq.shape[2], w_uq.shape[3]
    kl, nope, vd = w_uk.shape[1], w_uk.shape[3], w_uv.shape[3]
    rope = qk - nope
    e = w_router.shape[2]
    n_pages, ps = page_table.shape[1], cache_kv_latent.shape[2]
    past_len = n_pages * ps
    dims = dict(d=d, c=c, ql=ql, kl=kl, h=h, nope=nope, rope=rope, qk=qk, vd=vd, e=e)
    assert qk <= LANE and rope % 2 == 0 and e <= LANE and s >= width - 1
    assert all(v % LANE == 0 for v in (d, c, ql, kl)) and ts * h <= LANE

    n_p, n_s = b * s, bs * ts
    tabs_p = _rope_tables(jnp.arange(s, dtype=I32), nope, rope, qk)
    tabs_s = _rope_tables(jnp.tile(past_len + jnp.arange(ts, dtype=I32), bs), nope, rope, qk)
    tm_s = _tile(n_s, 512)
    ts_pad = -(-ts // (2 * SUBLANE)) * (2 * SUBLANE)

    yp = x_prompt.reshape(n_p, d)
    ys = x_sample.reshape(n_s, d)
    outs = [[] for _ in range(6)]
    for l in range(depth):
        wp = _prepare((g_norm_mix[l], g_norm_ffn[l], w_in[l], g_q_lat[l], w_uq[l], g_kv_lat[l], w_uk[l], w_uv[l],
                       g_q_head[l], g_k_head[l], w_attn_out[l], w_out[l], w_conv_out[l], w_router[l], b_router[l]),
                      dims, ts)
        ex = (w_up[l].astype(BF16), b_up[l], w_down[l].astype(BF16), b_down[l])
        mod = _adaln(jnp.concatenate([c_prompt, c_sample], axis=0), w_ada[l], b_ada[l]).reshape(b + bs, N_MOD, d)
        mod_p = [mod[:b, j].reshape(b, 1, d) for j in range(N_MOD)]
        mod_s = [jnp.repeat(mod[b:, j], ts, axis=0).reshape(n_s // tm_s, tm_s, d) for j in range(N_MOD)]

        def conv_p(v):
            yc, ns = _conv(v.reshape(b, s, c), jnp.zeros((b, width - 1, c), F32), w_dw[l], b_dw[l], g_conv_ln[l],
                           b_conv_ln[l], s)
            return yc.reshape(n_p, c), ns

        def conv_s(v):
            vp = jnp.pad(v.reshape(bs, ts, c), [(0, 0), (0, ts_pad - ts), (0, 0)])
            yc, ns = _conv(vp, state_conv[l], w_dw[l], b_dw[l], g_conv_ln[l], b_conv_ln[l], ts)
            return yc[:, :ts].reshape(n_s, c), ns

        def attend_p(q, ckv, kpe, k, vv):
            return _attention(q, k, vv, b, s, h, vd)

        def attend_s(q, ckv, kpe, k, vv):
            a = _absorb(q, wp["gkh"], wp["wukt"], wp["sel"], h, kl).reshape(bs, ts * h, kl + LANE)
            o = _paged_attention(a, ckv.reshape(bs, ts, kl), kpe.reshape(bs, ts, rope), cache_kv_latent[l],
                                 jnp.swapaxes(cache_k_rope[l], 1, 2), page_table, wp, dims)
            return o.reshape(n_s, h * vd)

        yp, c_p, r_p, s_p = _layer(yp, mod_p, tabs_p, conv_p, attend_p, wp, ex, dims, s)
        ys, c_s, r_s, s_s = _layer(ys, mod_s, tabs_s, conv_s, attend_s, wp, ex, dims, tm_s)
        for lst, val in zip(outs, (c_p.reshape(b, s, kl), r_p.reshape(b, s, rope), s_p, c_s.reshape(bs, ts, kl),
                                   r_s.reshape(bs, ts, rope), s_s)):
            lst.append(val)
    return (yp.reshape(b, s, d), ys.reshape(bs, ts, d)) + tuple(jnp.stack(o) for o in outs)
```

```python
import functools

import jax
import jax.numpy as jnp
from jax import lax
from jax.experimental import pallas as pl
from jax.experimental.pallas import tpu as pltpu

F32 = jnp.float32
BF16 = jnp.bfloat16
I32 = jnp.int32
U32 = jnp.uint32

EPS = 1e-6
TOP_K = 4
ROPE_THETA = 10000.0
SWIGLU_LIMIT = 7.0
SWIGLU_ALPHA = 1.702
N_MOD = 6

LANE = 128
SUBLANE = 8
VMEM_LIMIT_BYTES = 56 * 1024 * 1024
NEG = -1e30


def _cp(*sem):
    return pltpu.CompilerParams(dimension_semantics=sem, vmem_limit_bytes=VMEM_LIMIT_BYTES)


def _tile(n, pref, mult=SUBLANE):
    t = min(pref, n)
    t -= t % mult
    while t >= mult:
        if n % t == 0:
            return t
        t -= mult
    return n


def _row_tile(n):
    for t in (512, 256):
        if n % t == 0:
            return t
    return n


def _sigmoid(x):
    return 1.0 / (1.0 + jnp.exp(-x))


def _rms(x, g):
    return x * lax.rsqrt(jnp.mean(x * x, axis=-1, keepdims=True) + EPS) * g


def _dot(a, b):
    return jnp.dot(a, b, preferred_element_type=F32)


def _store_row_tiles(ref, x):
    r = x.shape[0]
    nc = x.shape[1] // LANE
    for c in range(nc):
        ref[pl.ds(c, r, stride=nc), :] = x[:, c * LANE:(c + 1) * LANE]


def _load_row_tiles(ref, nc):
    r = ref.shape[0] // nc
    return jnp.concatenate([ref[pl.ds(c, r, stride=nc), :] for c in range(nc)], axis=1)


def _adaln_kernel(c_ref, w_ref, b_ref, o_ref):
    c = c_ref[...]
    s = (c * _sigmoid(c)).astype(BF16)
    o_ref[...] = _dot(s, w_ref[...].astype(BF16)) + b_ref[...]


def _adaln(c, w_ada, b_ada):
    r, d = c.shape
    n = w_ada.shape[1]
    tn = _tile(n, 1536, LANE)
    return pl.pallas_call(
        _adaln_kernel,
        out_shape=jax.ShapeDtypeStruct((r, n), F32),
        grid=(n // tn,),
        in_specs=[
            pl.BlockSpec((r, d), lambda j: (0, 0)),
            pl.BlockSpec((d, tn), lambda j: (0, j)),
            pl.BlockSpec((1, tn), lambda j: (0, j)),
        ],
        out_specs=pl.BlockSpec((r, tn), lambda j: (0, j)),
        compiler_params=_cp("arbitrary"),
        name="adaln",
    )(c, w_ada, b_ada.reshape(1, n))


def _inproj_kernel(x_ref, sh_ref, sc_ref, cq_ref, sq_ref, ck_ref, sk_ref, gmix_ref, w1_ref, gql_ref, wq_ref,
                   gqh_ref, gkvl_ref, wuk_ref, epe_ref, gkh_ref, wuv_ref,
                   v_ref, q_ref, ckv_ref, kpe_ref, k_ref, vv_ref, *, c, ql, kl, h, qk, rope):
    x = x_ref[...]
    hmod = _rms(x, gmix_ref[...]) * (1.0 + sc_ref[0]) + sh_ref[0]
    p = _dot(hmod.astype(BF16), w1_ref[...])
    o_q = 2 * c
    o_kv = o_q + ql
    o_kr = o_kv + kl
    qn = _rms(p[:, o_q:o_kv], gql_ref[...])
    qq = _dot(qn.astype(BF16), wq_ref[...])
    ckv = _rms(p[:, o_kv:o_kr], gkvl_ref[...])
    kpe = p[:, o_kr:o_kr + LANE] * ck_ref[...] + p[:, o_kr + LANE:o_kr + 2 * LANE] * sk_ref[...]
    cb = ckv.astype(BF16)
    kf = _dot(cb, wuk_ref[...]) + _dot(kpe.astype(BF16), epe_ref[...])
    vv_ref[...] = _dot(cb, wuv_ref[...]).astype(vv_ref.dtype)
    v_ref[...] = p[:, :c] * _sigmoid(p[:, c:2 * c])
    ckv_ref[...] = ckv
    kpe_ref[...] = kpe[:, :rope]
    cq = cq_ref[...]
    sq = sq_ref[...]
    inv_qk = 1.0 / qk
    for hd in range(h):
        qr = qq[:, hd * LANE:(hd + 1) * LANE] * cq + qq[:, (h + hd) * LANE:(h + hd + 1) * LANE] * sq
        ms = jnp.sum(qr * qr, axis=-1, keepdims=True) * inv_qk
        q_ref[:, hd * LANE:(hd + 1) * LANE] = (qr * lax.rsqrt(ms + EPS) * gqh_ref[...]).astype(q_ref.dtype)
    for hd in range(h):
        kh = kf[:, hd * LANE:(hd + 1) * LANE]
        ms = jnp.sum(kh * kh, axis=-1, keepdims=True) * inv_qk
        k_ref[:, hd * LANE:(hd + 1) * LANE] = (kh * lax.rsqrt(ms + EPS) * gkh_ref[...]).astype(k_ref.dtype)


def _inproj(x, mods, mod_map, tabs, tab_map, wp, dims, tm):
    n, d = x.shape
    c, ql, kl, h, qk, rope, vd = dims["c"], dims["ql"], dims["kl"], dims["h"], dims["qk"], dims["rope"], dims["vd"]
    mblk = (1,) + mods[0].shape[1:]
    full = lambda a: pl.BlockSpec(a.shape, lambda i: (0,) * a.ndim)
    row = lambda w: pl.BlockSpec((tm, w), lambda i: (i, 0))
    tab = pl.BlockSpec((tm, LANE), tab_map)
    weights = [wp["gmix"], wp["w1"], wp["gql"], wp["wq"], wp["gqh"], wp["gkvl"], wp["wuk"], wp["epe"], wp["gkh"],
               wp["wuv"]]
    return pl.pallas_call(
        functools.partial(_inproj_kernel, c=c, ql=ql, kl=kl, h=h, qk=qk, rope=rope),
        out_shape=(
            jax.ShapeDtypeStruct((n, c), F32),
            jax.ShapeDtypeStruct((n, h * LANE), BF16),
            jax.ShapeDtypeStruct((n, kl), F32),
            jax.ShapeDtypeStruct((n, rope), F32),
            jax.ShapeDtypeStruct((n, h * LANE), BF16),
            jax.ShapeDtypeStruct((n, h * vd), BF16),
        ),
        grid=(n // tm,),
        in_specs=[row(d), pl.BlockSpec(mblk, mod_map), pl.BlockSpec(mblk, mod_map), tab, tab, tab, tab]
        + [full(w) for w in weights],
        out_specs=(row(c), row(h * LANE), row(kl), row(rope), row(h * LANE), row(h * vd)),
        compiler_params=_cp("parallel"),
        name="inproj",
    )(x, mods[0], mods[1], *tabs, *weights)


def _conv_kernel(v_ref, st_ref, wdw_ref, gln_ref, bln_ref, y_ref, ns_ref, win_ref, xs_ref, pre_ref, *, width, tt,
                 tv, rb, halo):
    j = pl.program_id(1)
    off = halo - (width - 1)
    nl = win_ref.shape[0]
    for lt in range(nl):
        ls = slice(lt * LANE, (lt + 1) * LANE)

        @pl.when(j == 0)
        def _():
            win_ref[lt, pl.ds(off, width - 1), :] = st_ref[0, :, ls]

        @pl.when(j > 0)
        def _():
            win_ref[lt, pl.ds(off, width - 1), :] = win_ref[lt, pl.ds(tt + off, width - 1), :]

        win_ref[lt, pl.ds(halo, tt), :] = v_ref[0, :, ls]
    gln = gln_ref[...]
    bln = bln_ref[...]
    span = rb + halo - SUBLANE

    def body(i, carry):
        r0 = pl.multiple_of(i * rb, rb)

        def lane_body(lt, inner):
            col = win_ref[lt, pl.ds(r0, rb + halo), :]
            acc = jnp.broadcast_to(wdw_ref[lt, width:width + 1, :], (rb, LANE))
            for s in range(SUBLANE):
                taps = [k for k in range(width) if (off + k) % SUBLANE == s]
                if not taps:
                    continue
                if s > 0:
                    xs_ref[0:span, :] = col[s:s + span, :]
                for k in taps:
                    a0 = (off + k) // SUBLANE * SUBLANE
                    xk = col[a0:a0 + rb, :] if s == 0 else xs_ref[a0:a0 + rb, :]
                    acc = acc + wdw_ref[lt, k:k + 1, :] * xk
            pre_ref[lt, pl.ds(r0, rb), :] = acc
            return inner

        lax.fori_loop(0, nl, lane_body, 0)
        return carry

    lax.fori_loop(0, tt // rb, body, 0)

    def norm_body(i, carry):
        r0 = pl.multiple_of(i * rb, rb)
        acc = jnp.concatenate([pre_ref[lt, pl.ds(r0, rb), :] for lt in range(nl)], axis=-1)
        mu = jnp.mean(acc, axis=-1, keepdims=True)
        dlt = acc - mu
        var = jnp.mean(dlt * dlt, axis=-1, keepdims=True)
        yn = dlt * lax.rsqrt(var + EPS) * gln + bln
        y_ref[0, pl.ds(r0, rb), :] = (yn * _sigmoid(yn)).astype(y_ref.dtype)
        return carry

    lax.fori_loop(0, tt // rb, norm_body, 0, unroll=min(4, tt // rb))

    @pl.when(j == pl.num_programs(1) - 1)
    def _():
        for lt in range(nl):
            ns_ref[0, :, lt * LANE:(lt + 1) * LANE] = win_ref[lt, pl.ds(tv + off, width - 1), :]


def _conv(v, state, w_dw, b_dw, g_ln, b_ln, t_valid):
    b, t, c = v.shape
    width = w_dw.shape[0]
    nl = c // LANE
    halo = -(-(width - 1) // SUBLANE) * SUBLANE
    tt = _tile(t, 512)
    rb = _tile(tt, 64, 2 * SUBLANE)
    nt = t // tt
    tv = t_valid - (nt - 1) * tt
    wp = -(-(width + 1) // SUBLANE) * SUBLANE
    wb = jnp.concatenate([w_dw, b_dw.reshape(1, c), jnp.zeros((wp - width - 1, c), F32)], axis=0)
    wb = wb.reshape(wp, nl, LANE).transpose(1, 0, 2)
    vec = lambda a: a.reshape(1, c)
    cst = lambda shape: pl.BlockSpec(shape, lambda i, j: (0,) * len(shape))
    return pl.pallas_call(
        functools.partial(_conv_kernel, width=width, tt=tt, tv=tv, rb=rb, halo=halo),
        out_shape=(jax.ShapeDtypeStruct((b, t, c), BF16), jax.ShapeDtypeStruct((b, width - 1, c), F32)),
        grid=(b, nt),
        in_specs=[
            pl.BlockSpec((1, tt, c), lambda i, j: (i, j, 0)),
            pl.BlockSpec((1, width - 1, c), lambda i, j: (i, 0, 0)),
            cst((nl, wp, LANE)), cst((1, c)), cst((1, c)),
        ],
        out_specs=(
            pl.BlockSpec((1, tt, c), lambda i, j: (i, j, 0)),
            pl.BlockSpec((1, width - 1, c), lambda i, j: (i, 0, 0)),
        ),
        scratch_shapes=[pltpu.VMEM((nl, tt + halo, LANE), F32), pltpu.VMEM((rb + halo, LANE), F32),
                        pltpu.VMEM((nl, tt, LANE), F32)],
        compiler_params=_cp("parallel", "arbitrary"),
        name="conv",
    )(v, state, wb, vec(g_ln), vec(b_ln))


def _attn_kernel(q_ref, k_ref, v_ref, o_ref, *, tq, tk, vd):
    qi = pl.program_id(2)
    q = q_ref[0]
    per = tq // tk
    krow = lax.broadcasted_iota(I32, (tk, tq), 0)
    qcol = lax.broadcasted_iota(I32, (tk, tq), 1)

    def block(j, carry, diag):
        ks = pl.multiple_of(j * tk, tk)
        kb = k_ref[0, pl.ds(ks, tk), :]
        vb = v_ref[0, pl.ds(ks, tk), :]
        sts = [lax.dot_general(kb[:, hh * LANE:(hh + 1) * LANE], q[:, hh * LANE:(hh + 1) * LANE],
                               (((1,), (1,)), ((), ())), preferred_element_type=F32) for hh in range(2)]
        out = []
        for hh in range(2):
            m, l, acc = carry[hh]
            st = sts[hh]
            if diag is not None:
                st = jnp.where(krow + diag * tk <= qcol, st, NEG)
            m_new = jnp.maximum(m, jnp.max(st, axis=0, keepdims=True))
            alpha = jnp.exp(m - m_new)
            pt = jnp.exp(st - m_new)
            l = alpha * l + jnp.sum(pt, axis=0, keepdims=True)
            pv = lax.dot_general(vb, pt.astype(BF16), (((0,), (0,)), ((), ())), preferred_element_type=F32)
            out.append((m_new, l, alpha * acc + pv))
        return tuple(out)

    one = (jnp.full((1, tq), NEG, F32), jnp.zeros((1, tq), F32), jnp.zeros((LANE, tq), F32))
    carry = lax.fori_loop(0, qi * per, lambda j, cr: block(j, cr, None), (one, one))
    for u in range(per):
        carry = block(qi * per + u, carry, u)
    o0 = carry[0][2] / carry[0][1]
    o1 = carry[1][2] / carry[1][1]
    vrow = lax.broadcasted_iota(I32, (LANE, tq), 0)
    o_ref[0] = jnp.where(vrow < vd, o0, o1).T.astype(o_ref.dtype)


def _attention(q, k, v, b, s, h, vd):
    assert 2 * vd == LANE and h % 2 == 0
    tq = _tile(s, 512)
    tk = _tile(tq, 512)
    q3 = q.reshape(b, s, h * LANE)
    k3 = k.reshape(b, s, h * LANE)
    v3 = v.reshape(b, s, h * vd)
    out = pl.pallas_call(
        functools.partial(_attn_kernel, tq=tq, tk=tk, vd=vd),
        out_shape=jax.ShapeDtypeStruct((b, s, h * vd), BF16),
        grid=(b, h // 2, s // tq),
        in_specs=[
            pl.BlockSpec((1, tq, 2 * LANE), lambda i, p, j: (i, j, p)),
            pl.BlockSpec((1, s, 2 * LANE), lambda i, p, j: (i, 0, p)),
            pl.BlockSpec((1, s, LANE), lambda i, p, j: (i, 0, p)),
        ],
        out_specs=pl.BlockSpec((1, tq, LANE), lambda i, p, j: (i, j, p)),
        compiler_params=_cp("parallel", "parallel", "arbitrary"),
        name="attention",
    )(q3, k3, v3)
    return out.reshape(b * s, h * vd)


def _absorb_kernel(q_ref, gkh_ref, wukt_ref, sel_ref, a_ref, *, h, kl):
    aw = kl + LANE
    for hd in range(h):
        qg = (q_ref[:, hd * LANE:(hd + 1) * LANE].astype(F32) * gkh_ref[...]).astype(BF16)
        a_ref[:, hd * aw:hd * aw + kl] = _dot(qg, wukt_ref[hd])
        a_ref[:, hd * aw + kl:(hd + 1) * aw] = _dot(qg, sel_ref[...])


def _absorb(q, gkh_pad, wukt, sel, h, kl):
    n = q.shape[0]
    full = lambda a: pl.BlockSpec(a.shape, lambda i: (0,) * a.ndim)
    return pl.pallas_call(
        functools.partial(_absorb_kernel, h=h, kl=kl),
        out_shape=jax.ShapeDtypeStruct((n, h * (kl + LANE)), F32),
        grid=(1,),
        in_specs=[full(q), full(gkh_pad), full(wukt), full(sel)],
        out_specs=pl.BlockSpec((n, h * (kl + LANE)), lambda i: (0, 0)),
        compiler_params=_cp("arbitrary"),
        name="absorb",
    )(q, gkh_pad, wukt, sel)


def _paged_kernel(pt_ref, a_ref, cn_ref, pn_ref, wukt_ref, sel_ref, wuv_ref, poolc_ref, poolp_ref, o_ref,
                  cbuf, pbuf, sems, lhs_sc, m_sc, l_sc, acc_sc, *, pg, kl, rope, qk, h, nope, vd, t):
    j = pl.program_id(1)
    nj = pl.num_programs(1)
    step = pl.program_id(0) * nj + j
    last = pl.num_programs(0) * nj - 1
    slot = step % 2

    def page_copies(stp, sl):
        out = []
        for u in range(pg):
            page = pt_ref[stp * pg + u]
            out.append(pltpu.make_async_copy(poolc_ref.at[page], cbuf.at[sl, u], sems.at[0, sl]))
            out.append(pltpu.make_async_copy(poolp_ref.at[page], pbuf.at[sl, u], sems.at[1, sl]))
        return out

    def start_all(copies):
        for n_cp, cp in enumerate(copies):
            cp.start(priority=(n_cp // 2) % 2)

    @pl.when(step == 0)
    def _():
        start_all(page_copies(0, 0))

    start_all(page_copies(jnp.minimum(step + 1, last), 1 - slot))
    pltpu.make_async_copy(poolc_ref.at[pl.ds(0, pg)], cbuf.at[slot], sems.at[0, slot]).wait()
    pltpu.make_async_copy(poolp_ref.at[pl.ds(0, pg)], pbuf.at[slot], sems.at[1, slot]).wait()
    th = t * h
    hn = h * nope
    a = a_ref[0]
    a_p = a[:, kl:kl + rope].astype(BF16)
    inv_qk = 1.0 / qk

    @pl.when(j == 0)
    def _():
        lhs_sc[0:hn, :] = wukt_ref[...]
        lhs_sc[hn:hn + th, :] = a[:, :kl].astype(BF16)
        m_sc[...] = jnp.full(m_sc.shape, NEG, F32)
        l_sc[...] = jnp.zeros(l_sc.shape, F32)
        acc_sc[...] = jnp.zeros(acc_sc.shape, F32)

    def fold(x):
        out = x[0:SUBLANE]
        for r in range(1, x.shape[0] // SUBLANE):
            out = out + x[r * SUBLANE:(r + 1) * SUBLANE]
        return out

    def project(cb):
        return lax.dot_general(lhs_sc[...], cb, (((1,), (1,)), ((), ())), preferred_element_type=F32)

    def scores(res, num_rope, rope_sq):
        parts = []
        for hd in range(h):
            blk = res[hd * nope:(hd + 1) * nope, :]
            parts.append(fold(blk * blk))
        parts.append(rope_sq)
        parts.append(jnp.zeros_like(parts[0]))
        stack = jnp.concatenate(parts, axis=0)
        hi = stack.astype(BF16)
        lo = (stack - hi.astype(F32)).astype(BF16)
        ksq = _dot(sel_ref[...], hi) + _dot(sel_ref[...], lo)
        rinv = lax.rsqrt(ksq * inv_qk + EPS)
        num = res[hn:hn + th, :] + num_rope
        return jnp.concatenate([num[tt * h:(tt + 1) * h, :] * rinv for tt in range(t)], axis=0)

    def update(cf, ptf, valid):
        num_rope = _dot(a_p, ptf.astype(BF16))
        rope_sq = fold(ptf * ptf)
        cb = cf.astype(BF16)
        s = scores(project(cb), num_rope, rope_sq)
        if valid is not None:
            s = jnp.where(valid, s, NEG)
        m_old = m_sc[:, 0:1]
        m_new = jnp.maximum(m_old, jnp.max(s, axis=-1, keepdims=True))
        alpha = jnp.exp(m_old - m_new)
        p = jnp.exp(s - m_new)
        l_sc[...] = alpha * l_sc[...] + jnp.sum(p, axis=-1, keepdims=True)
        m_sc[...] = jnp.broadcast_to(m_new, m_sc.shape)
        acc_sc[...] = alpha * acc_sc[...] + _dot(p.astype(BF16), cb)

    ps = cbuf.shape[2]
    cf = cbuf[slot].reshape(pg * ps, kl)
    ptf = jnp.concatenate([pbuf[slot, u] for u in range(pg)], axis=1)
    update(cf, ptf, None)

    @pl.when(step == last)
    def _():
        pltpu.make_async_copy(poolc_ref.at[pl.ds(0, pg)], cbuf.at[1 - slot], sems.at[0, 1 - slot]).wait()
        pltpu.make_async_copy(poolp_ref.at[pl.ds(0, pg)], pbuf.at[1 - slot], sems.at[1, 1 - slot]).wait()

    @pl.when(j == nj - 1)
    def _():
        tn = cn_ref.shape[1]
        kcol = lax.broadcasted_iota(I32, (th, tn), 1)
        qrow = lax.broadcasted_iota(I32, (th, tn), 0) // h
        update(cn_ref[0], pn_ref[0], jnp.logical_and(kcol <= qrow, kcol < t))
        on = (acc_sc[...] / l_sc[:, 0:1]).astype(BF16)
        oa = _dot(on, wuv_ref[...])
        r_h = lax.broadcasted_iota(I32, (th, h * vd), 0) % h
        c_h = lax.broadcasted_iota(I32, (th, h * vd), 1) // vd
        oa = jnp.where(r_h == c_h, oa, 0.0)
        rows = [jnp.sum(oa[tt * h:(tt + 1) * h, :], axis=0, keepdims=True) for tt in range(t)]
        o_ref[0] = jnp.concatenate(rows, axis=0)


def _paged_attention(a, c_new, pe_new, pool_c, pool_pet, page_table, wp, dims):
    bs, th, aw = a.shape
    t = c_new.shape[1]
    kl, rope, qk, h, vd, nope = dims["kl"], dims["rope"], dims["qk"], dims["h"], dims["vd"], dims["nope"]
    assert h == SUBLANE and nope % SUBLANE == 0 and rope % SUBLANE == 0 and t <= LANE
    c_new = jnp.pad(c_new, [(0, 0), (0, LANE - t), (0, 0)])
    pet_new = jnp.pad(jnp.swapaxes(pe_new, 1, 2), [(0, 0), (0, 0), (0, LANE - t)])
    n_pages = page_table.shape[1]
    ps = pool_c.shape[1]
    pg = _tile(n_pages, 64, 1)
    pt = page_table.reshape(-1).astype(I32)

    cst = lambda arr: pl.BlockSpec(arr.shape, lambda i, j, ptr: (0,) * arr.ndim)
    in_specs = [
        pl.BlockSpec((1, th, aw), lambda i, j, ptr: (i, 0, 0)),
        pl.BlockSpec((1, LANE, kl), lambda i, j, ptr: (i, 0, 0)),
        pl.BlockSpec((1, rope, LANE), lambda i, j, ptr: (i, 0, 0)),
        cst(wp["wukt_flat"]), cst(wp["sel_sq"]), cst(wp["wuv"]),
        pl.BlockSpec(memory_space=pl.ANY), pl.BlockSpec(memory_space=pl.ANY),
    ]
    grid_spec = pltpu.PrefetchScalarGridSpec(
        num_scalar_prefetch=1,
        grid=(bs, n_pages // pg),
        in_specs=in_specs,
        out_specs=pl.BlockSpec((1, t, h * vd), lambda i, j, ptr: (i, 0, 0)),
        scratch_shapes=[pltpu.VMEM((2, pg, ps, kl), F32), pltpu.VMEM((2, pg, rope, ps), F32),
                        pltpu.SemaphoreType.DMA((2, 2)),
                        pltpu.VMEM((h * nope + th, kl), BF16), pltpu.VMEM((th, LANE), F32),
                        pltpu.VMEM((th, LANE), F32), pltpu.VMEM((th, kl), F32)],
    )
    return pl.pallas_call(
        functools.partial(_paged_kernel, pg=pg, kl=kl, rope=rope, qk=qk, h=h, nope=nope, vd=vd, t=t),
        out_shape=jax.ShapeDtypeStruct((bs, t, h * vd), F32),
        grid_spec=grid_spec,
        compiler_params=_cp("arbitrary", "arbitrary"),
        name="paged_attention",
    )(pt, a, c_new, pet_new, wp["wukt_flat"], wp["sel_sq"], wp["wuv"], pool_c, pool_pet)


def _mixer_kernel(x_ref, yc_ref, at_ref, sh1_ref, sc1_ref, g1_ref, sh2_ref, sc2_ref, gmix_ref, wg_ref, wco_ref,
                  wao_ref, wout_ref, gffn_ref, wrh_ref, wrl_ref, br_ref, tri_ref,
                  xm_ref, h2_ref, idx_ref, prob_ref, pos_ref, cnt_ref, run_ref, *, d, e):
    conv_out = _dot(yc_ref[...].astype(BF16), wco_ref[...])
    attn_out = _dot(at_ref[...].astype(BF16), wao_ref[...])
    x = x_ref[...]
    hmod = _rms(x, gmix_ref[...]) * (1.0 + sc1_ref[0]) + sh1_ref[0]
    gates = _dot(hmod.astype(BF16), wg_ref[...])
    merged = _sigmoid(gates[:, :d]) * conv_out + _sigmoid(gates[:, d:]) * attn_out
    xm = x + g1_ref[0] * _dot(merged.astype(BF16), wout_ref[...])
    xm_ref[...] = xm
    h2 = _rms(xm, gffn_ref[...]) * (1.0 + sc2_ref[0]) + sh2_ref[0]
    _store_row_tiles(h2_ref, h2)
    hi = h2.astype(BF16)
    lo = (h2 - hi.astype(F32)).astype(BF16)
    lg = _dot(hi, wrh_ref[...]) + _dot(lo, wrh_ref[...]) + _dot(hi, wrl_ref[...]) + br_ref[...]
    tm = x.shape[0]
    lane = lax.broadcasted_iota(I32, (tm, LANE), 1)
    lane_f = lane.astype(F32)
    lg = jnp.where(lane < e, lg, -jnp.inf)
    vals, idxs = [], []
    for _ in range(TOP_K):
        m = jnp.max(lg, axis=-1, keepdims=True)
        ik = jnp.min(jnp.where(lg == m, lane_f, float(LANE)), axis=-1, keepdims=True)
        vals.append(m)
        idxs.append(ik)
        lg = jnp.where(lane_f == ik, -jnp.inf, lg)
    ex = [jnp.exp(vk - vals[0]) for vk in vals]
    den = ex[0]
    for ek in ex[1:]:
        den = den + ek
    @pl.when(pl.program_id(0) == 0)
    def _():
        run_ref[...] = jnp.zeros(run_ref.shape, F32)

    ohs = [lane_f == ik for ik in idxs]
    sel = ohs[0].astype(F32)
    for oh in ohs[1:]:
        sel = sel + oh.astype(F32)
    before = _dot(tri_ref[...], sel.astype(BF16)) + run_ref[0:1, :]
    idx_o = jnp.zeros((tm, LANE), F32)
    prob_o = jnp.zeros((tm, LANE), F32)
    pos_o = jnp.zeros((tm, LANE), F32)
    for k in range(TOP_K):
        idx_o = jnp.where(lane == k, idxs[k], idx_o)
        prob_o = jnp.where(lane == k, ex[k] / den, prob_o)
        pos_o = jnp.where(lane == k, jnp.sum(jnp.where(ohs[k], before, 0.0), axis=-1, keepdims=True), pos_o)
    idx_ref[...] = idx_o.astype(I32)
    prob_ref[...] = prob_o
    pos_ref[...] = pos_o.astype(I32)
    run_ref[...] = run_ref[...] + jnp.sum(sel, axis=0, keepdims=True)
    cnt_ref[...] = run_ref[...]


def _mixer(x, yc, at, mods, mod_map, wp, dims, tm):
    n, d = x.shape
    e = dims["e"]
    tri = (jnp.arange(tm)[None, :] < jnp.arange(tm)[:, None]).astype(BF16)
    mblk = (1,) + mods[0].shape[1:]
    full = lambda a: pl.BlockSpec(a.shape, lambda i: (0,) * a.ndim)
    row = lambda w: pl.BlockSpec((tm, w), lambda i: (i, 0))
    mspec = pl.BlockSpec(mblk, mod_map)
    weights = [wp["gmix"], wp["wg"], wp["wco"], wp["wao"], wp["wout"], wp["gffn"], wp["wrh"], wp["wrl"], wp["br"],
               tri]
    return pl.pallas_call(
        functools.partial(_mixer_kernel, d=d, e=e),
        out_shape=(
            jax.ShapeDtypeStruct((n, d), F32),
            jax.ShapeDtypeStruct((n * (d // LANE), LANE), F32),
            jax.ShapeDtypeStruct((n, LANE), I32),
            jax.ShapeDtypeStruct((n, LANE), F32),
            jax.ShapeDtypeStruct((n, LANE), I32),
            jax.ShapeDtypeStruct((SUBLANE, LANE), F32),
        ),
        grid=(n // tm,),
        in_specs=[row(d), row(yc.shape[1]), row(at.shape[1]), mspec, mspec, mspec, mspec, mspec]
        + [full(w) for w in weights],
        out_specs=(row(d), pl.BlockSpec((tm * (d // LANE), LANE), lambda i: (i, 0)), row(LANE), row(LANE),
                   row(LANE), pl.BlockSpec((SUBLANE, LANE), lambda i: (0, 0))),
        scratch_shapes=[pltpu.VMEM((SUBLANE, LANE), F32)],
        compiler_params=_cp("arbitrary"),
        name="mixer",
    )(x, yc, at, mods[0], mods[1], mods[2], mods[3], mods[4], *weights)


def _lane_cumsum(x):
    lane = lax.broadcasted_iota(I32, x.shape, 1)
    s = 1
    while s < LANE:
        x = x + jnp.where(lane >= s, pltpu.roll(x, s, axis=1), 0.0)
        s *= 2
    return x


def _route_kernel(idx_ref, pos_ref, cnt_ref, dest_ref, meta_ref, *, e, blk, nbp, tm):
    i = pl.program_id(0)
    cnt = cnt_ref[...]
    gend = _lane_cumsum(cnt)
    gstart = gend - cnt

    @pl.when(i == 0)
    def _():
        inv = 1.0 / blk
        first_b = jnp.floor(gstart * inv)
        last_b = jnp.floor((gend - 1.0) * inv)
        n_it = jnp.where(cnt > 0.0, last_b - first_b + 1.0, 0.0)
        it_end = _lane_cumsum(n_it)
        it_off = it_end - n_it
        lane_w = lax.broadcasted_iota(I32, (nbp, LANE), 1)
        w = lax.broadcasted_iota(I32, (nbp, LANE), 0).astype(F32)[:, 0:1]
        ex = jnp.sum(jnp.where(jnp.logical_and(lane_w < e, it_end[0:1, :] <= w), 1.0, 0.0), axis=-1, keepdims=True)
        ex = jnp.minimum(ex, float(e - 1))
        ohw = lane_w.astype(F32) == ex
        pick = lambda v: jnp.sum(jnp.where(ohw, v[0:1, :], 0.0), axis=-1, keepdims=True)
        bk = pick(first_b) + (w - pick(it_off))
        lo = jnp.maximum(pick(gstart), bk * blk) - bk * blk
        hi = jnp.minimum(pick(gend), (bk + 1.0) * blk) - bk * blk
        n_items = jnp.max(it_end[0:1, :], axis=-1, keepdims=True)
        meta = jnp.zeros((nbp, LANE), F32)
        for col, val in enumerate((ex, bk, lo, hi, n_items)):
            meta = jnp.where(lane_w == col, val, meta)
        meta_ref[...] = meta.astype(I32)

    lane = lax.broadcasted_iota(I32, (tm, LANE), 1)
    idx = idx_ref[...]
    pos = pos_ref[...].astype(F32)
    dest = jnp.zeros((tm, LANE), F32)
    for k in range(TOP_K):
        gk = jnp.sum(jnp.where(lane == idx[:, k:k + 1], gstart[0:1, :], 0.0), axis=-1, keepdims=True)
        dest = jnp.where(lane == k, gk + pos[:, k:k + 1], dest)
    dest_ref[...] = dest.astype(I32)


def _route(idx, pos, cnt, e, blk, nbp):
    n = idx.shape[0]
    tm = _tile(n, 1024)
    return pl.pallas_call(
        functools.partial(_route_kernel, e=e, blk=blk, nbp=nbp, tm=tm),
        out_shape=(jax.ShapeDtypeStruct((n, LANE), I32), jax.ShapeDtypeStruct((nbp, LANE), I32)),
        grid=(n // tm,),
        in_specs=[pl.BlockSpec((tm, LANE), lambda i: (i, 0)), pl.BlockSpec((tm, LANE), lambda i: (i, 0)),
                  pl.BlockSpec((SUBLANE, LANE), lambda i: (0, 0))],
        out_specs=(pl.BlockSpec((tm, LANE), lambda i: (i, 0)), pl.BlockSpec((nbp, LANE), lambda i: (0, 0))),
        compiler_params=_cp("arbitrary"),
        name="route",
    )(idx, pos, cnt)


def _dispatch_kernel(dest_ref, h_ref, buf_ref, sem, *, tm, nc):
    def body(g, carry):
        t0 = pl.multiple_of(g * SUBLANE, SUBLANE)
        for u in range(SUBLANE):
            for k in range(TOP_K):
                dst = dest_ref[t0 * TOP_K + (u * TOP_K + k)]
                src = pl.ds(pl.multiple_of((t0 + u) * nc, nc), nc)
                pltpu.make_async_copy(h_ref.at[src, :], buf_ref.at[pl.ds(pl.multiple_of(dst * nc, nc), nc), :],
                                      sem).start(priority=k % 2)
        return carry

    lax.fori_loop(0, tm // SUBLANE, body, 0)
    for k in range(TOP_K):
        pltpu.make_async_copy(h_ref, buf_ref.at[pl.ds(0, tm * nc), :], sem).wait()


def _dispatch(h2, dest_flat, rows, nc):
    n = h2.shape[0] // nc
    tm = _row_tile(n)
    return pl.pallas_call(
        functools.partial(_dispatch_kernel, tm=tm, nc=nc),
        out_shape=jax.ShapeDtypeStruct((rows * nc, LANE), h2.dtype),
        grid=(n // tm,),
        in_specs=[pl.BlockSpec((tm * TOP_K,), lambda i: (i,), memory_space=pltpu.SMEM),
                  pl.BlockSpec((tm * nc, LANE), lambda i: (i, 0))],
        out_specs=pl.BlockSpec(memory_space=pl.ANY),
        scratch_shapes=[pltpu.SemaphoreType.DMA(())],
        compiler_params=_cp("arbitrary"),
        name="dispatch",
    )(dest_flat, h2)


def _expert_kernel(ex_ref, bk_ref, lo_ref, hi_ref, ni_ref, x_ref, wup_ref, bup_ref, wdn_ref, bdn_ref, y_ref,
                   yacc_ref, *, blk, f):
    w = pl.program_id(0)
    del ex_ref

    @pl.when(w == 0)
    def _():
        yacc_ref[...] = jnp.zeros(yacc_ref.shape, F32)

    @pl.when(w < ni_ref[0])
    def _():
        rowi = lax.broadcasted_iota(I32, (blk, 1), 0)
        inside = jnp.logical_and(rowi >= lo_ref[w], rowi < hi_ref[w])
        nc = wup_ref.shape[1] // LANE
        x = jnp.where(inside, _load_row_tiles(x_ref, nc), 0.0).astype(BF16)
        gu = _dot(x, wup_ref[0]) + bup_ref[0]
        gate = jnp.minimum(gu[:, :f], SWIGLU_LIMIT)
        lin = jnp.clip(gu[:, f:], -SWIGLU_LIMIT, SWIGLU_LIMIT)
        act = gate * _sigmoid(SWIGLU_ALPHA * gate) * (lin + 1.0)
        y = _dot(act.astype(BF16), wdn_ref[0]) + bdn_ref[0]
        first = jnp.logical_or(w == 0, bk_ref[w] != bk_ref[jnp.maximum(w - 1, 0)])
        y = jnp.where(inside, y, jnp.where(first, 0.0, yacc_ref[...]))
        yacc_ref[...] = y
        _store_row_tiles(y_ref, y)


def _experts(buf, meta, w_up, b_up, w_dn, b_dn, blk, n_items_max):
    e, d, f2 = w_up.shape
    nc = d // LANE
    f = f2 // 2
    ex, bk, lo, hi = (meta[:, c] for c in range(4))
    ni = meta[:1, 4]

    def item(w, ni_r):
        return jnp.minimum(w, ni_r[0] - 1)

    def blk_map(w, ex_r, bk_r, lo_r, hi_r, ni_r):
        return (bk_r[item(w, ni_r)], 0)

    def exp_map(w, ex_r, bk_r, lo_r, hi_r, ni_r):
        return (ex_r[item(w, ni_r)], 0, 0)

    grid_spec = pltpu.PrefetchScalarGridSpec(
        num_scalar_prefetch=5,
        grid=(n_items_max,),
        in_specs=[
            pl.BlockSpec((blk * nc, LANE), blk_map),
            pl.BlockSpec((1, d, f2), exp_map),
            pl.BlockSpec((1, 1, f2), exp_map),
            pl.BlockSpec((1, f, d), exp_map),
            pl.BlockSpec((1, 1, d), exp_map),
        ],
        out_specs=pl.BlockSpec((blk * nc, LANE), blk_map),
        scratch_shapes=[pltpu.VMEM((blk, d), F32)],
    )
    return pl.pallas_call(
        functools.partial(_expert_kernel, blk=blk, f=f),
        out_shape=jax.ShapeDtypeStruct(buf.shape, F32),
        grid_spec=grid_spec,
        compiler_params=_cp("arbitrary"),
        name="experts",
    )(ex, bk, lo, hi, ni, buf, w_up, b_up.reshape(e, 1, f2), w_dn, b_dn.reshape(e, 1, d))


def _combine_kernel(dcur_ref, dnxt_ref, x_ref, g2_ref, p_ref, y_ref, o_ref, yg_ref, sem, *, tm, nc):
    i = pl.program_id(0)
    n = pl.num_programs(0)
    slot = i % 2

    def issue(dref, s):
        def body(g, carry):
            t0 = pl.multiple_of(g * SUBLANE, SUBLANE)
            for u in range(SUBLANE):
                for k in range(TOP_K):
                    src = dref[t0 * TOP_K + (u * TOP_K + k)]
                    pltpu.make_async_copy(y_ref.at[pl.ds(pl.multiple_of(src * nc, nc), nc), :],
                                          yg_ref.at[s, k, pl.ds(pl.multiple_of((t0 + u) * nc, nc), nc), :],
                                          sem.at[s]).start(priority=k % 2)
            return carry

        lax.fori_loop(0, tm // SUBLANE, body, 0)

    @pl.when(i == 0)
    def _():
        issue(dcur_ref, 0)

    @pl.when(i + 1 < n)
    def _():
        issue(dnxt_ref, 1 - slot)

    for k in range(TOP_K):
        pltpu.make_async_copy(y_ref.at[pl.ds(0, tm * nc), :], yg_ref.at[slot, k], sem.at[slot]).wait()
    prob = p_ref[...]
    acc = prob[:, 0:1] * _load_row_tiles(yg_ref.at[slot, 0], nc)
    for k in range(1, TOP_K):
        acc = acc + prob[:, k:k + 1] * _load_row_tiles(yg_ref.at[slot, k], nc)
    o_ref[...] = x_ref[...] + g2_ref[0] * acc


def _combine(xm, g2, mod_map, prob, ybuf, dest_flat):
    n, d = xm.shape
    nc = d // LANE
    tm = _row_tile(n)
    nt = n // tm
    mblk = (1,) + g2.shape[1:]
    return pl.pallas_call(
        functools.partial(_combine_kernel, tm=tm, nc=nc),
        out_shape=jax.ShapeDtypeStruct((n, d), F32),
        grid=(nt,),
        in_specs=[
            pl.BlockSpec((tm * TOP_K,), lambda i: (i,), memory_space=pltpu.SMEM),
            pl.BlockSpec((tm * TOP_K,), lambda i: (jnp.minimum(i + 1, nt - 1),), memory_space=pltpu.SMEM),
            pl.BlockSpec((tm, d), lambda i: (i, 0)),
            pl.BlockSpec(mblk, mod_map),
            pl.BlockSpec((tm, LANE), lambda i: (i, 0)),
            pl.BlockSpec(memory_space=pl.ANY),
        ],
        out_specs=pl.BlockSpec((tm, d), lambda i: (i, 0)),
        scratch_shapes=[pltpu.VMEM((2, TOP_K, tm * nc, LANE), F32), pltpu.SemaphoreType.DMA((2,))],
        compiler_params=_cp("arbitrary"),
        name="combine",
    )(dest_flat, dest_flat, xm, g2, prob, ybuf)


def _rope_tables(pos, nope, rope, qk):
    p = pos.shape[0]
    inv_freq = ROPE_THETA ** (-jnp.arange(0, rope, 2, dtype=F32) / rope)
    ang = pos.astype(F32)[:, None] * inv_freq
    c2 = jnp.concatenate([jnp.cos(ang)] * 2, axis=-1)
    s2 = jnp.concatenate([jnp.sin(ang)] * 2, axis=-1)
    cq = jnp.concatenate([jnp.ones((p, nope), F32), c2, jnp.ones((p, LANE - qk), F32)], axis=-1)
    sq = jnp.concatenate([jnp.zeros((p, nope), F32), s2, jnp.zeros((p, LANE - qk), F32)], axis=-1)
    ck = jnp.concatenate([c2, jnp.zeros((p, LANE - rope), F32)], axis=-1)
    sk = jnp.concatenate([s2, jnp.zeros((p, LANE - rope), F32)], axis=-1)
    return cq, sq, ck, sk


def _swap_halves(w):
    half = w.shape[-1] // 2
    return jnp.concatenate([-w[..., half:], w[..., :half]], axis=-1)


def _pad_last(w, n):
    return jnp.pad(w, [(0, 0)] * (w.ndim - 1) + [(0, n - w.shape[-1])])


def _prepare(p, dims, t_s):
    (g_mix, g_ffn, w_in, g_ql, w_uq, g_kvl, w_uk, w_uv, g_qh, g_kh, w_ao, w_out, w_co, w_r, b_r) = p
    d, c, ql, kl, h = dims["d"], dims["c"], dims["ql"], dims["kl"], dims["h"]
    nope, rope, qk, vd, e = dims["nope"], dims["rope"], dims["qk"], dims["vd"], dims["e"]
    o_q = 2 * c
    o_kv = o_q + ql
    o_kr = o_kv + kl
    o_ga = o_kr + rope
    w_kr = w_in[:, o_kr:o_ga]
    w1 = jnp.concatenate([w_in[:, :o_kr], _pad_last(w_kr, LANE), _pad_last(_swap_halves(w_kr), LANE)], axis=1)
    wq_plain = _pad_last(w_uq, LANE)
    wq_swap = jnp.concatenate([jnp.zeros((ql, h, nope), F32), _swap_halves(w_uq[..., nope:]),
                               jnp.zeros((ql, h, LANE - qk), F32)], axis=-1)
    wq = jnp.concatenate([wq_plain.reshape(ql, h * LANE), wq_swap.reshape(ql, h * LANE)], axis=1)
    sm_scale = qk ** -0.5
    place = jnp.zeros((LANE, LANE), F32).at[jnp.arange(rope), nope + jnp.arange(rope)].set(1.0)
    epe = jnp.tile(place, (1, h))
    wukt = _pad_last(jnp.transpose(w_uk, (1, 2, 0)), kl)
    wukt = jnp.pad(wukt, [(0, 0), (0, LANE - nope), (0, 0)])
    sel = jnp.zeros((LANE, LANE), F32).at[nope + jnp.arange(rope), jnp.arange(rope)].set(1.0)
    col = jnp.arange((h + 2) * SUBLANE) // SUBLANE
    sel_sq = jnp.logical_or(col[None, :] == jnp.arange(h)[:, None], col[None, :] == h).astype(BF16)
    hi = w_r.astype(BF16)
    lo = (w_r - hi.astype(F32)).astype(BF16)
    row = lambda g: g.reshape(1, -1).astype(F32)
    return dict(
        gmix=row(g_mix), gffn=row(g_ffn), w1=w1.astype(BF16), gql=row(g_ql), wq=wq.astype(BF16),
        gqh=row(_pad_last(g_qh, LANE)) * sm_scale, gkvl=row(g_kvl),
        wuk=_pad_last(w_uk, LANE).reshape(kl, h * LANE).astype(BF16), epe=epe.astype(BF16),
        gkh=row(_pad_last(g_kh, LANE)), wuv=w_uv.reshape(kl, h * vd).astype(BF16),
        wukt=wukt.astype(BF16), sel=sel.astype(BF16), wukt_flat=w_uk.reshape(kl, h * nope).T.astype(BF16),
        sel_sq=sel_sq,
        wg=w_in[:, o_ga:].astype(BF16), wco=w_co.astype(BF16), wao=w_ao.astype(BF16), wout=w_out.astype(BF16),
        wrh=_pad_last(hi, LANE), wrl=_pad_last(lo, LANE), br=row(_pad_last(b_r, LANE)),
    )


def _moe(xm, h2, idx, prob, pos, cnt, g2, mod_map, w_up, b_up, w_dn, b_dn, e):
    n = xm.shape[0]
    n_asg = n * TOP_K
    blk = 512 if n_asg >= 2048 * e else 128
    assert n_asg % blk == 0
    n_items_max = n_asg // blk + e - 1
    nbp = -(-n_items_max // SUBLANE) * SUBLANE
    dest, meta = _route(idx, pos, cnt, e, blk, nbp)
    dest_flat = dest[:, :TOP_K].reshape(-1)
    buf = _dispatch(h2, dest_flat, n_asg, xm.shape[1] // LANE)
    ybuf = _experts(buf, meta, w_up, b_up, w_dn, b_dn, blk, n_items_max)
    return _combine(xm, g2, mod_map, prob, ybuf, dest_flat)


def _layer(x, mod, pos_tabs, conv_state, attend, wp, ex, dims, group):
    n, d = x.shape
    tm = _tile(n, 512)
    if mod[0].shape[1] == 1:
        per = group // tm
        mod_map = lambda i: (i // per, 0, 0)
        tab_map = lambda i: (i % per, 0)
    else:
        mod_map = lambda i: (i, 0, 0)
        tab_map = lambda i: (i, 0)
    shift1, scale1, gate1, shift2, scale2, gate2 = mod
    v, q, ckv, kpe, k, vv = _inproj(x, (shift1, scale1), mod_map, pos_tabs, tab_map, wp, dims, tm)
    yc, new_state = conv_state(v)
    at = attend(q, ckv, kpe, k, vv)
    xm, h2, idx, prob, pos, cnt = _mixer(x, yc, at, (shift1, scale1, gate1, shift2, scale2), mod_map, wp, dims, tm)
    tmc = _row_tile(n)
    if mod[0].shape[1] == 1:
        perc = group // tmc
        comb_map = lambda i: (i // perc, 0, 0)
        g2 = gate2
    else:
        comb_map = lambda i: (i, 0, 0)
        g2 = gate2.reshape(n // tmc, tmc, d)
    y = _moe(xm, h2, idx, prob, pos, cnt, g2, comb_map, *ex, dims["e"])
    return y, ckv, kpe, new_state


def kernel(x_prompt, x_sample, cache_kv_latent, cache_k_rope, state_conv, page_table, c_prompt, c_sample, w_ada, b_ada, g_norm_mix, g_norm_ffn, w_in, w_dw, b_dw, g_conv_ln, b_conv_ln, w_conv_out, g_q_lat, w_uq, g_kv_lat, w_uk, w_uv, g_q_head, g_k_head, w_attn_out, w_out, w_router, b_router, w_up, b_up, w_down, b_down):
    b, s, d = x_prompt.shape
    bs, ts, _ = x_sample.shape
    depth = w_ada.shape[0]
    width, c = w_dw.shape[1], w_dw.shape[2]
    ql, h, qk = w_uq.shape[1], w_uq.shape[2], w_uq.shape[3]
    kl, nope, vd = w_uk.shape[1], w_uk.shape[3], w_uv.shape[3]
    rope = qk - nope
    e = w_router.shape[2]
    n_pages, ps = page_table.shape[1], cache_kv_latent.shape[2]
    past_len = n_pages * ps
    dims = dict(d=d, c=c, ql=ql, kl=kl, h=h, nope=nope, rope=rope, qk=qk, vd=vd, e=e)
    assert qk <= LANE and rope % 2 == 0 and e <= LANE and s >= width - 1
    assert all(v % LANE == 0 for v in (d, c, ql, kl)) and ts * h <= LANE

    n_p, n_s = b * s, bs * ts
    tabs_p = _rope_tables(jnp.arange(s, dtype=I32), nope, rope, qk)
    tabs_s = _rope_tables(jnp.tile(past_len + jnp.arange(ts, dtype=I32), bs), nope, rope, qk)
    tm_s = _tile(n_s, 512)
    ts_pad = -(-ts // (2 * SUBLANE)) * (2 * SUBLANE)

    yp = x_prompt.reshape(n_p, d)
    ys = x_sample.reshape(n_s, d)
    outs = [[] for _ in range(6)]
    for l in range(depth):
        wp = _prepare((g_norm_mix[l], g_norm_ffn[l], w_in[l], g_q_lat[l], w_uq[l], g_kv_lat[l], w_uk[l], w_uv[l],
                       g_q_head[l], g_k_head[l], w_attn_out[l], w_out[l], w_conv_out[l], w_router[l], b_router[l]),
                      dims, ts)
        ex = (w_up[l].astype(BF16), b_up[l], w_down[l].astype(BF16), b_down[l])
        mod = _adaln(jnp.concatenate([c_prompt, c_sample], axis=0), w_ada[l], b_ada[l]).reshape(b + bs, N_MOD, d)
        mod_p = [mod[:b, j].reshape(b, 1, d) for j in range(N_MOD)]
        mod_s = [jnp.repeat(mod[b:, j], ts, axis=0).reshape(n_s // tm_s, tm_s, d) for j in range(N_MOD)]

        def conv_p(v):
            yc, ns = _conv(v.reshape(b, s, c), jnp.zeros((b, width - 1, c), F32), w_dw[l], b_dw[l], g_conv_ln[l],
                           b_conv_ln[l], s)
            return yc.reshape(n_p, c), ns

        def conv_s(v):
            vp = jnp.pad(v.reshape(bs, ts, c), [(0, 0), (0, ts_pad - ts), (0, 0)])
            yc, ns = _conv(vp, state_conv[l], w_dw[l], b_dw[l], g_conv_ln[l], b_conv_ln[l], ts)
            return yc[:, :ts].reshape(n_s, c), ns

        def attend_p(q, ckv, kpe, k, vv):
            return _attention(q, k, vv, b, s, h, vd)

        def attend_s(q, ckv, kpe, k, vv):
            a = _absorb(q, wp["gkh"], wp["wukt"], wp["sel"], h, kl).reshape(bs, ts * h, kl + LANE)
            o = _paged_attention(a, ckv.reshape(bs, ts, kl), kpe.reshape(bs, ts, rope), cache_kv_latent[l],
                                 jnp.swapaxes(cache_k_rope[l], 1, 2), page_table, wp, dims)
            return o.reshape(n_s, h * vd)

        yp, c_p, r_p, s_p = _layer(yp, mod_p, tabs_p, conv_p, attend_p, wp, ex, dims, s)
        ys, c_s, r_s, s_s = _layer(ys, mod_s, tabs_s, conv_s, attend_s, wp, ex, dims, tm_s)
        for lst, val in zip(outs, (c_p.reshape(b, s, kl), r_p.reshape(b, s, rope), s_p, c_s.reshape(bs, ts, kl),
                                   r_s.reshape(bs, ts, rope), s_s)):
            lst.append(val)
    return (yp.reshape(b, s, d), ys.reshape(bs, ts, d)) + tuple(jnp.stack(o) for o in outs)
```
